```python
import math
import jax, jax.numpy as jnp
from jax import lax
import numpy as np

D_MODEL = 1024
BATCH = 32
SEQ = 256
DEPTH = 4
DEC_BATCH = 2
DEC_SEQ = 1024
PAST_LEN = 512

GRID_W = 64
N_MIXERS = 3
N_A = (DEPTH + 2) // 3
N_B = (DEPTH + 1) // 3
N_C = DEPTH // 3
N_MOD = 6
EPS = 1e-6
D_RNN = 1280
LRU_BLOCKS = 10
LRU_BLOCK = D_RNN // LRU_BLOCKS
LRU_CONV_W = 4
LRU_CONV_LEFT = 2
LRU_C = 8.0
LRU_A_MIN = 0.9
LRU_A_MAX = 0.999
D_B = 2 * D_MODEL
CHUNK = 128
G_B = 8
N_HEADS_C = 8
HEAD_DIM_C = D_MODEL // (2 * N_HEADS_C)
N_FREQ_AXIS = HEAD_DIM_C // 4
ROPE_BASE = 10000.0
Q_BLOCK = 128
D_FF = 2816
FFN_CONV_W = 3
FFN_CONV_LEFT = 1

kernel_name = 'hybrid_diffusion_rglru_chunkmlp_diffattn_step'


def rms_norm(x, g):
    xf = x.astype(jnp.float32)
    y = xf * lax.rsqrt(jnp.mean(xf * xf, axis=-1, keepdims=True) + EPS)
    return (y * g.astype(jnp.float32)).astype(x.dtype)


def adaln(cond, w, b):
    m = jax.nn.silu(cond) @ w + b
    return jnp.split(m[:, None, :], N_MOD, axis=-1)


def modulate(x, g, shift, scale):
    return rms_norm(x, g) * (1 + scale) + shift


def dw_conv(x, w, b, left):
    K = w.shape[0]
    T = x.shape[1]
    xp = jnp.pad(x, ((0, 0), (left, K - 1 - left), (0, 0)))
    y = b
    for k in range(K):
        y = y + xp[:, k:k + T] * w[k]
    return y


def linear_scan(a, b, h0):
    b = b.at[:, 0].add(a[:, 0] * h0)
    def combine(l, r):
        al, bl = l
        ar, br = r
        return al * ar, ar * bl + br
    _, h = lax.associative_scan(combine, (a, b), axis=1)
    return h


def rglru_mixer(h, h0, w_in, conv_w, conv_b, w_gate, b_gate, lam, w_out):
    gate_br, rec = jnp.split(h @ w_in, 2, axis=-1)
    xc = dw_conv(rec, conv_w, conv_b, LRU_CONV_LEFT)
    B, T, _ = xc.shape
    xb = xc.reshape(B, T, LRU_BLOCKS, LRU_BLOCK)
    gates = jax.nn.sigmoid(jnp.einsum('btnk,dgnkj->dgbtnj', xb, w_gate) + b_gate[:, :, None, None])
    gates = gates.astype(jnp.float32).reshape(2, 2, B, T, D_RNN)
    r, i = gates[:, 0], gates[:, 1]
    log_a = -LRU_C * r * jax.nn.softplus(-lam.astype(jnp.float32))[:, None, None, :]
    a = jnp.exp(log_a)
    bx = jnp.sqrt(-jnp.expm1(2.0 * log_a)) * (i * xc.astype(jnp.float32)[None])
    h0f = h0.astype(jnp.float32)
    hf = linear_scan(a[0], bx[0], h0f[:, 0])
    hb = jnp.flip(linear_scan(jnp.flip(a[1], 1), jnp.flip(bx[1], 1), h0f[:, 1]), 1)
    y = (jax.nn.gelu(gate_br) * (hf + hb).astype(h.dtype)) @ w_out
    final = jnp.stack([hf[:, -1], hb[:, 0]], axis=1).astype(h.dtype)
    return y, final


def chunk_mlp_mixer(h, w_in, b_in, norm_g, w_s, b_s, w_out):
    u, v = jnp.split(jax.nn.gelu(h @ w_in + b_in), 2, axis=-1)
    v = rms_norm(v, norm_g)
    B, T, _ = v.shape
    vr = v.reshape(B, T // CHUNK, CHUNK, G_B, D_B // G_B)
    sv = jnp.einsum('gpq,bcqgk->bcpgk', w_s, vr) + b_s.T[:, :, None]
    return (u * sv.reshape(B, T, D_B)) @ w_out


def axial_rope(T):
    rows = T // GRID_W
    row = jnp.repeat(jnp.arange(rows), GRID_W)
    col = jnp.tile(jnp.arange(GRID_W), rows)
    inv = ROPE_BASE ** (-jnp.arange(N_FREQ_AXIS, dtype=jnp.float32) / N_FREQ_AXIS)
    ang = jnp.concatenate([row[:, None] * inv, col[:, None] * inv], axis=-1)
    return jnp.cos(ang), jnp.sin(ang)


def apply_rope(x, cos, sin):
    half = HEAD_DIM_C // 2
    x1, x2 = x[..., :half], x[..., half:]
    cs = cos[None, :, None, None, :].astype(x.dtype)
    sn = sin[None, :, None, None, :].astype(x.dtype)
    return jnp.concatenate([x1 * cs - x2 * sn, x2 * cs + x1 * sn], axis=-1)


def diff_qkv(h, w_qkv, qk_g, rope):
    B, T, _ = h.shape
    q, k, v = jnp.split(h @ w_qkv, 3, axis=-1)
    q = rms_norm(q.reshape(B, T, N_HEADS_C, 2, HEAD_DIM_C), qk_g[0])
    k = rms_norm(k.reshape(B, T, N_HEADS_C, 2, HEAD_DIM_C), qk_g[1])
    v = v.reshape(B, T, N_HEADS_C, 2 * HEAD_DIM_C)
    if rope is not None:
        q = apply_rope(q, rope[0], rope[1])
        k = apply_rope(k, rope[0], rope[1])
    return q, k, v


def diff_lambda(lam_vec, lam_init):
    lv = lam_vec.astype(jnp.float32)
    return jnp.exp(jnp.sum(lv[0] * lv[1])) - jnp.exp(jnp.sum(lv[2] * lv[3])) + lam_init


def diff_attention(q, k, v, lam, lam_init, subln_g):
    B, Tq = q.shape[0], q.shape[1]
    nb = Tq // Q_BLOCK
    qb = q.reshape(B, nb, Q_BLOCK, N_HEADS_C, 2, HEAD_DIM_C).transpose(1, 0, 2, 3, 4, 5)
    scale = HEAD_DIM_C ** -0.5
    def one_block(qblk):
        s = jnp.einsum('bqhcd,bkhcd->bchqk', qblk, k).astype(jnp.float32) * scale
        p = jax.nn.softmax(s, axis=-1)
        w = p[:, 0] - lam * p[:, 1]
        return jnp.einsum('bhqk,bkhe->bqhe', w.astype(v.dtype), v)
    o = lax.map(one_block, qb)
    o = o.transpose(1, 0, 2, 3, 4).reshape(B, Tq, N_HEADS_C, 2 * HEAD_DIM_C)
    o = rms_norm(o, subln_g) * (1.0 - lam_init)
    return o.reshape(B, Tq, N_HEADS_C * 2 * HEAD_DIM_C)


def conv_ffn(h, w_up, conv_w, conv_b, w_down):
    z = dw_conv(h @ w_up, conv_w, conv_b, FFN_CONV_LEFT)
    g, u = jnp.split(z, 2, axis=-1)
    return (jax.nn.silu(g) * u) @ w_down


def setup_inputs(seed: int = 0) -> dict:
    key = jax.random.key(seed)
    ks = iter(jax.random.split(key, 40))
    f32 = jnp.float32
    def nrm(shape, scale):
        return jax.random.normal(next(ks), shape, f32) * scale
    def gain(shape):
        return 1.0 + nrm(shape, 0.02)
    a0 = jax.random.uniform(next(ks), (N_A, 2, D_RNN), f32, minval=LRU_A_MIN, maxval=LRU_A_MAX)
    a_root = a0 ** (1.0 / LRU_C)
    lru_lambda = jnp.log(a_root) - jnp.log1p(-a_root)
    return {
        'x_prompt': nrm((BATCH, SEQ, D_MODEL), 1.0),
        'x_sample': nrm((DEC_BATCH, DEC_SEQ, D_MODEL), 1.0),
        'state_lru': nrm((DEC_BATCH, N_A, 2, D_RNN), 0.5),
        'cache_k': nrm((DEC_BATCH, N_C, PAST_LEN, N_HEADS_C, 2, HEAD_DIM_C), 1.0),
        'cache_v': nrm((DEC_BATCH, N_C, PAST_LEN, N_HEADS_C, 2 * HEAD_DIM_C), 1.0),
        'c': nrm((DEC_BATCH, D_MODEL), 1.0),
        'c_ctx': nrm((D_MODEL,), 1.0),
        'w_mod': nrm((DEPTH, D_MODEL, N_MOD * D_MODEL), 0.5 * D_MODEL ** -0.5),
        'b_mod': nrm((DEPTH, N_MOD * D_MODEL), 0.02),
        'norm_g': gain((DEPTH, 2, D_MODEL)),
        'lru_w_in': nrm((N_A, D_MODEL, 2 * D_RNN), D_MODEL ** -0.5),
        'lru_conv_w': nrm((N_A, LRU_CONV_W, D_RNN), LRU_CONV_W ** -0.5),
        'lru_conv_b': nrm((N_A, D_RNN), 0.02),
        'lru_w_gate': nrm((N_A, 2, 2, LRU_BLOCKS, LRU_BLOCK, LRU_BLOCK), LRU_BLOCK ** -0.5),
        'lru_b_gate': nrm((N_A, 2, 2, LRU_BLOCKS, LRU_BLOCK), 0.02),
        'lru_lambda': lru_lambda,
        'lru_w_out': nrm((N_A, D_RNN, D_MODEL), D_RNN ** -0.5),
        'cmlp_w_in': nrm((N_B, D_MODEL, 2 * D_B), D_MODEL ** -0.5),
        'cmlp_b_in': nrm((N_B, 2 * D_B), 0.02),
        'cmlp_norm_g': gain((N_B, D_B)),
        'cmlp_w_s': nrm((N_B, G_B, CHUNK, CHUNK), CHUNK ** -0.5),
        'cmlp_b_s': nrm((N_B, G_B, CHUNK), 0.02),
        'cmlp_w_out': nrm((N_B, D_B, D_MODEL), D_B ** -0.5),
        'attn_w_qkv': nrm((N_C, D_MODEL, 3 * N_HEADS_C * 2 * HEAD_DIM_C), D_MODEL ** -0.5),
        'attn_qk_g': gain((N_C, 2, HEAD_DIM_C)),
        'attn_lambda': nrm((N_C, 4, HEAD_DIM_C), 0.1),
        'attn_subln_g': gain((N_C, 2 * HEAD_DIM_C)),
        'attn_w_out': nrm((N_C, N_HEADS_C * 2 * HEAD_DIM_C, D_MODEL), (N_HEADS_C * 2 * HEAD_DIM_C) ** -0.5),
        'ffn_w_up': nrm((DEPTH, D_MODEL, 2 * D_FF), D_MODEL ** -0.5),
        'ffn_conv_w': nrm((DEPTH, FFN_CONV_W, 2 * D_FF), FFN_CONV_W ** -0.5),
        'ffn_conv_b': nrm((DEPTH, 2 * D_FF), 0.02),
        'ffn_w_down': nrm((DEPTH, D_FF, D_MODEL), D_FF ** -0.5),
    }


def reference(x_prompt, x_sample, state_lru, cache_k, cache_v, c, c_ctx, w_mod, b_mod, norm_g,
              lru_w_in, lru_conv_w, lru_conv_b, lru_w_gate, lru_b_gate, lru_lambda, lru_w_out,
              cmlp_w_in, cmlp_b_in, cmlp_norm_g, cmlp_w_s, cmlp_b_s, cmlp_w_out,
              attn_w_qkv, attn_qk_g, attn_lambda, attn_subln_g, attn_w_out,
              ffn_w_up, ffn_conv_w, ffn_conv_b, ffn_w_down):
    xp, xs = x_prompt, x_sample
    rope = axial_rope(xs.shape[1])
    cond_ctx = c_ctx[None, :]
    new_lru, new_k, new_v = [], [], []
    for l in range(DEPTH):
        kind = l % N_MIXERS
        j = l // N_MIXERS
        sh1p, sc1p, g1p, sh2p, sc2p, g2p = adaln(cond_ctx, w_mod[l], b_mod[l])
        sh1s, sc1s, g1s, sh2s, sc2s, g2s = adaln(c, w_mod[l], b_mod[l])
        hp = modulate(xp, norm_g[l, 0], sh1p, sc1p)
        hs = modulate(xs, norm_g[l, 0], sh1s, sc1s)
        if kind == 0:
            lru_args = (lru_w_in[j], lru_conv_w[j], lru_conv_b[j], lru_w_gate[j], lru_b_gate[j],
                        lru_lambda[j], lru_w_out[j])
            h0 = jnp.zeros((xp.shape[0], 2, D_RNN), xp.dtype)
            yp, st = rglru_mixer(hp, h0, *lru_args)
            ys, _ = rglru_mixer(hs, state_lru[:, j], *lru_args)
            new_lru.append(st)
        elif kind == 1:
            cm_args = (cmlp_w_in[j], cmlp_b_in[j], cmlp_norm_g[j], cmlp_w_s[j], cmlp_b_s[j], cmlp_w_out[j])
            yp = chunk_mlp_mixer(hp, *cm_args)
            ys = chunk_mlp_mixer(hs, *cm_args)
        else:
            lam_init = 0.8 - 0.6 * math.exp(-0.3 * l)
            lam = diff_lambda(attn_lambda[j], lam_init)
            qp, kp, vp = diff_qkv(hp, attn_w_qkv[j], attn_qk_g[j], None)
            qs, ks_lat, vs_lat = diff_qkv(hs, attn_w_qkv[j], attn_qk_g[j], rope)
            yp = diff_attention(qp, kp, vp, lam, lam_init, attn_subln_g[j]) @ attn_w_out[j]
            k_all = jnp.concatenate([cache_k[:, j], ks_lat], axis=1)
            v_all = jnp.concatenate([cache_v[:, j], vs_lat], axis=1)
            ys = diff_attention(qs, k_all, v_all, lam, lam_init, attn_subln_g[j]) @ attn_w_out[j]
            new_k.append(kp)
            new_v.append(vp)
        xp = xp + g1p * yp
        xs = xs + g1s * ys
        hp = modulate(xp, norm_g[l, 1], sh2p, sc2p)
        hs = modulate(xs, norm_g[l, 1], sh2s, sc2s)
        xp = xp + g2p * conv_ffn(hp, ffn_w_up[l], ffn_conv_w[l], ffn_conv_b[l], ffn_w_down[l])
        xs = xs + g2s * conv_ffn(hs, ffn_w_up[l], ffn_conv_w[l], ffn_conv_b[l], ffn_w_down[l])
    new_state_lru = jnp.stack(new_lru, axis=1)
    new_cache_k = jnp.stack(new_k, axis=1)
    new_cache_v = jnp.stack(new_v, axis=1)
    return (xp, xs, new_state_lru, new_cache_k, new_cache_v)
```

```python
import functools
import math

import jax
import jax.numpy as jnp
from jax import lax
from jax.experimental import pallas as pl
from jax.experimental.pallas import tpu as pltpu

D_MODEL = 1024
BATCH = 32
SEQ = 256
DEPTH = 4
DEC_BATCH = 2
DEC_SEQ = 1024
PAST_LEN = 512
GRID_W = 64
N_MOD = 6
EPS = 1e-6
D_RNN = 1280
LRU_BLOCKS = 10
LRU_BLOCK = 128
LRU_CONV_W = 4
LRU_CONV_LEFT = 2
LRU_C = 8.0
D_B = 2 * D_MODEL
CHUNK = 128
G_B = 8
N_HEADS_C = 8
HEAD_DIM_C = 64
N_FREQ_AXIS = 16
ROPE_BASE = 10000.0
D_FF = 2816
FFN_CONV_W = 3

N_PROMPT = BATCH * SEQ
N_SAMPLE = DEC_BATCH * DEC_SEQ
N_ROWS = N_PROMPT + N_SAMPLE
TM = 1024
N_TILES = N_ROWS // TM
N_PROMPT_TILES = N_PROMPT // TM
SEQ_PER_TILE = TM // SEQ
SUBLANES = 8
LANES = 128
FF_TN = 256
N_FF_CHUNKS = D_FF // FF_TN
TM_B = 512
TM_Q = 512
TQ = 256
VMEM_LIMIT = 56 * 1024 * 1024

BF16 = jnp.bfloat16
F32 = jnp.float32


def _dot(a, b):
    return jnp.dot(a, b, preferred_element_type=F32)


def _dot_nt(a, b):
    return lax.dot_general(a, b, (((1,), (1,)), ((), ())), preferred_element_type=F32)


def _mod_row(i):
    return jnp.maximum(i - (N_PROMPT_TILES - 1), 0)


def _modulate(x, ng, sh, sc):
    ms = jnp.mean(x * x, axis=-1, keepdims=True)
    return (x * lax.rsqrt(ms + EPS) * ng) * (1.0 + sc) + sh


def _gelu(x):
    k = math.sqrt(2.0 / math.pi)
    return 0.5 * x * (1.0 + jnp.tanh(k * (x + 0.044715 * (x * x * x))))


def _sigmoid(x):
    return 1.0 / (1.0 + jnp.exp(-x))


def _params(sem):
    return pltpu.CompilerParams(dimension_semantics=sem, vmem_limit_bytes=VMEM_LIMIT)


def _adaln_kernel(cond_ref, w_ref, b_ref, o_ref):
    cnd = cond_ref[...]
    s = (cnd * _sigmoid(cnd)).astype(BF16)
    o_ref[...] = _dot(s, w_ref[...].astype(BF16)) + b_ref[...]


def _adaln(cond8, w_mod, b_mod):
    out = pl.pallas_call(
        _adaln_kernel,
        grid=(DEPTH, N_MOD),
        in_specs=[
            pl.BlockSpec((SUBLANES, D_MODEL), lambda l, k: (0, 0)),
            pl.BlockSpec((None, D_MODEL, D_MODEL), lambda l, k: (l, 0, k)),
            pl.BlockSpec((None, None, 1, D_MODEL), lambda l, k: (l, k, 0, 0)),
        ],
        out_specs=pl.BlockSpec((None, None, SUBLANES, D_MODEL), lambda l, k: (l, k, 0, 0)),
        out_shape=jax.ShapeDtypeStruct((DEPTH, N_MOD, SUBLANES, D_MODEL), F32),
        compiler_params=_params(("arbitrary", "arbitrary")),
        name="adaln",
    )(cond8, w_mod, b_mod.reshape(DEPTH, N_MOD, 1, D_MODEL))
    return out.reshape(DEPTH, N_MOD, SUBLANES, 1, D_MODEL)


def _mod_spec(layer, k, tile_rows=TM):
    per = TM // tile_rows
    return pl.BlockSpec((None, None, None, 1, D_MODEL),
                        lambda i, *_: (layer, k, _mod_row(i // per), 0, 0))


def _ffn_kernel(x_ref, sh_ref, sc_ref, gt_ref, ng_ref, wg_ref, wu_ref, cwg_ref, cwu_ref,
                cbg_ref, cbu_ref, wd_ref, o_ref, h_scr, acc_scr):
    i = pl.program_id(0)
    j = pl.program_id(1)

    @pl.when(j == 0)
    def _():
        h = _modulate(x_ref[...], ng_ref[...], sh_ref[...], sc_ref[...])
        h_scr[...] = h.astype(BF16)
        acc_scr[...] = jnp.zeros_like(acc_scr)

    h = h_scr[...]
    zg = _dot(h, wg_ref[...].astype(BF16))
    zu = _dot(h, wu_ref[...].astype(BF16))
    seq_len = jnp.where(i < N_PROMPT_TILES, SEQ, DEC_SEQ)
    pos = lax.broadcasted_iota(jnp.int32, (TM, FF_TN), 0) & (seq_len - 1)
    first = pos == 0
    last = pos == seq_len - 1

    def conv(z, cw_ref, cb_ref):
        prev = jnp.where(first, 0.0, pltpu.roll(z, 1, 0))
        nxt = jnp.where(last, 0.0, pltpu.roll(z, TM - 1, 0))
        return cb_ref[...] + prev * cw_ref[0:1, :] + z * cw_ref[1:2, :] + nxt * cw_ref[2:3, :]

    g = conv(zg, cwg_ref, cbg_ref)
    u = conv(zu, cwu_ref, cbu_ref)
    act = (g * _sigmoid(g)) * u
    acc_scr[...] += _dot(act.astype(BF16), wd_ref[...].astype(BF16))

    @pl.when(j == N_FF_CHUNKS - 1)
    def _():
        o_ref[...] = x_ref[...] + gt_ref[...] * acc_scr[...]


def _ffn(x, mod, layer, ng, w_up, conv_w, conv_b, w_down):
    conv_b2 = conv_b.reshape(1, 2 * D_FF)
    return pl.pallas_call(
        _ffn_kernel,
        grid=(N_TILES, N_FF_CHUNKS),
        in_specs=[
            pl.BlockSpec((TM, D_MODEL), lambda i, j: (i, 0)),
            _mod_spec(layer, 3), _mod_spec(layer, 4), _mod_spec(layer, 5),
            pl.BlockSpec((1, D_MODEL), lambda i, j: (0, 0)),
            pl.BlockSpec((D_MODEL, FF_TN), lambda i, j: (0, j)),
            pl.BlockSpec((D_MODEL, FF_TN), lambda i, j: (0, N_FF_CHUNKS + j)),
            pl.BlockSpec((FFN_CONV_W, FF_TN), lambda i, j: (0, j)),
            pl.BlockSpec((FFN_CONV_W, FF_TN), lambda i, j: (0, N_FF_CHUNKS + j)),
            pl.BlockSpec((1, FF_TN), lambda i, j: (0, j)),
            pl.BlockSpec((1, FF_TN), lambda i, j: (0, N_FF_CHUNKS + j)),
            pl.BlockSpec((FF_TN, D_MODEL), lambda i, j: (j, 0)),
        ],
        out_specs=pl.BlockSpec((TM, D_MODEL), lambda i, j: (i, 0)),
        out_shape=jax.ShapeDtypeStruct((N_ROWS, D_MODEL), F32),
        scratch_shapes=[pltpu.VMEM((TM, D_MODEL), BF16), pltpu.VMEM((TM, D_MODEL), F32)],
        compiler_params=_params(("arbitrary", "arbitrary")),
        name=f"ffn{layer}",
    )(x, mod, mod, mod, ng.reshape(1, D_MODEL), w_up, w_up, conv_w, conv_w, conv_b2, conv_b2, w_down)


def _lru_kernel(x_ref, sh_ref, sc_ref, gt_ref, ng_ref, win_ref, cw_ref, cb_ref, wgt_ref, bgt_ref,
                lam_ref, h0_ref, wout_ref, o_ref, st_ref,
                h_scr, yg_scr, af_scr, bf_scr, ab_scr, bb_scr, hf_scr, hb_scr):
    i = pl.program_id(0)
    n = pl.program_id(1)

    @pl.when(n == 0)
    def _():
        h = _modulate(x_ref[...], ng_ref[...], sh_ref[...], sc_ref[...])
        h_scr[...] = h.astype(BF16)

    zz = _dot(h_scr[...], win_ref[...])
    gate_br = zz[:, :LRU_BLOCK]
    rec = zz[:, LRU_BLOCK:]

    seq_len = jnp.where(i < N_PROMPT_TILES, SEQ, DEC_SEQ)
    row = lax.broadcasted_iota(jnp.int32, (TM, LRU_BLOCK), 0)
    pos = row & (seq_len - 1)
    xc = cb_ref[...]
    for k in range(LRU_CONV_W):
        off = k - LRU_CONV_LEFT
        if off == 0:
            tap = rec
        else:
            rolled = pltpu.roll(rec, (-off) % TM, 0)
            ok = (pos + off >= 0) & (pos + off < seq_len)
            tap = jnp.where(ok, rolled, 0.0)
        xc = xc + tap * cw_ref[k:k + 1, :]

    gates = _sigmoid(_dot(xc.astype(BF16), wgt_ref[...]) + bgt_ref[...])
    lam = lam_ref[...]
    softplus = jnp.maximum(-lam, 0.0) + jnp.log1p(jnp.exp(-jnp.abs(lam)))

    rmod = row & (SUBLANES - 1)

    def coeffs(d):
        r = gates[:, (2 * d) * LRU_BLOCK:(2 * d + 1) * LRU_BLOCK]
        ig = gates[:, (2 * d + 1) * LRU_BLOCK:(2 * d + 2) * LRU_BLOCK]
        log_a = (-LRU_C * r) * softplus[d:d + 1, :]
        a = jnp.exp(log_a)
        bx = jnp.sqrt(-jnp.tanh(log_a) * (a * a + 1.0)) * (ig * xc)
        return a, bx

    a, b = coeffs(0)
    for s in (1, 2, 4):
        m = rmod >= s
        a_s = jnp.where(m, pltpu.roll(a, s, 0), 1.0)
        b_s = jnp.where(m, pltpu.roll(b, s, 0), 0.0)
        b = a * b_s + b
        a = a * a_s
    af_scr[...] = a
    bf_scr[...] = b
    a, b = coeffs(1)
    for s in (1, 2, 4):
        m = rmod < SUBLANES - s
        a_s = jnp.where(m, pltpu.roll(a, TM - s, 0), 1.0)
        b_s = jnp.where(m, pltpu.roll(b, TM - s, 0), 0.0)
        b = a * b_s + b
        a = a * a_s
    ab_scr[...] = a
    bb_scr[...] = b

    def chains(n_seq, t_len, init_f, init_b):
        n_grp = t_len // SUBLANES

        def body(g, carry):
            cf, cb = carry
            new_f, new_b = [], []
            for q in range(n_seq):
                rf = pl.multiple_of(q * t_len + g * SUBLANES, SUBLANES)
                hf = af_scr[pl.ds(rf, SUBLANES), :] * cf[q] + bf_scr[pl.ds(rf, SUBLANES), :]
                hf_scr[pl.ds(rf, SUBLANES), :] = hf
                new_f.append(hf[SUBLANES - 1:SUBLANES, :])
                rb = pl.multiple_of(q * t_len + (n_grp - 1 - g) * SUBLANES, SUBLANES)
                hb = ab_scr[pl.ds(rb, SUBLANES), :] * cb[q] + bb_scr[pl.ds(rb, SUBLANES), :]
                hb_scr[pl.ds(rb, SUBLANES), :] = hb
                new_b.append(hb[0:1, :])
            return tuple(new_f), tuple(new_b)

        return lax.fori_loop(0, n_grp, body, (tuple(init_f), tuple(init_b)))

    @pl.when(i < N_PROMPT_TILES)
    def _():
        zero = jnp.zeros((1, LRU_BLOCK), F32)
        cf, cb = chains(SEQ_PER_TILE, SEQ, [zero] * SEQ_PER_TILE, [zero] * SEQ_PER_TILE)
        for q in range(SEQ_PER_TILE):
            st_ref[q, 0:1, :] = cf[q]
            st_ref[q, 1:2, :] = cb[q]

    @pl.when(i >= N_PROMPT_TILES)
    def _():
        chains(1, DEC_SEQ, [h0_ref[0:1, :]], [h0_ref[1:2, :]])
        st_ref[...] = jnp.zeros_like(st_ref)

    yg = _gelu(gate_br) * (hf_scr[...] + hb_scr[...])
    col = pl.multiple_of(n * LRU_BLOCK, LRU_BLOCK)
    yg_scr[:, pl.ds(col, LRU_BLOCK)] = yg.astype(BF16)

    @pl.when(n == LRU_BLOCKS - 1)
    def _():
        o_ref[...] = x_ref[...] + gt_ref[...] * _dot(yg_scr[...], wout_ref[...])


def _lru_mixer(x, mod, layer, ng, h0, w_in, conv_w, conv_b, w_gate, b_gate, lam, w_out):
    w_in_r = w_in.reshape(D_MODEL, 2, LRU_BLOCKS, LRU_BLOCK).transpose(2, 0, 1, 3)
    w_in_r = w_in_r.reshape(LRU_BLOCKS, D_MODEL, 2 * LRU_BLOCK).astype(BF16)
    w_gate_r = w_gate.transpose(2, 3, 0, 1, 4).reshape(LRU_BLOCKS, LRU_BLOCK, 4 * LRU_BLOCK).astype(BF16)
    b_gate_r = b_gate.transpose(2, 0, 1, 3).reshape(LRU_BLOCKS, 1, 4 * LRU_BLOCK)
    out, st = pl.pallas_call(
        _lru_kernel,
        grid=(N_TILES, LRU_BLOCKS),
        in_specs=[
            pl.BlockSpec((TM, D_MODEL), lambda i, n: (i, 0)),
            _mod_spec(layer, 0), _mod_spec(layer, 1), _mod_spec(layer, 2),
            pl.BlockSpec((1, D_MODEL), lambda i, n: (0, 0)),
            pl.BlockSpec((None, D_MODEL, 2 * LRU_BLOCK), lambda i, n: (n, 0, 0)),
            pl.BlockSpec((LRU_CONV_W, LRU_BLOCK), lambda i, n: (0, n)),
            pl.BlockSpec((1, LRU_BLOCK), lambda i, n: (0, n)),
            pl.BlockSpec((None, LRU_BLOCK, 4 * LRU_BLOCK), lambda i, n: (n, 0, 0)),
            pl.BlockSpec((None, 1, 4 * LRU_BLOCK), lambda i, n: (n, 0, 0)),
            pl.BlockSpec((2, LRU_BLOCK), lambda i, n: (0, n)),
            pl.BlockSpec((None, 2, LRU_BLOCK), lambda i, n: (jnp.maximum(i - N_PROMPT_TILES, 0), 0, n)),
            pl.BlockSpec((D_RNN, D_MODEL), lambda i, n: (0, 0)),
        ],
        out_specs=[
            pl.BlockSpec((TM, D_MODEL), lambda i, n: (i, 0)),
            pl.BlockSpec((SEQ_PER_TILE, 2, LRU_BLOCK), lambda i, n: (i, 0, n)),
        ],
        out_shape=[
            jax.ShapeDtypeStruct((N_ROWS, D_MODEL), F32),
            jax.ShapeDtypeStruct((N_TILES * SEQ_PER_TILE, 2, D_RNN), F32),
        ],
        scratch_shapes=[pltpu.VMEM((TM, D_MODEL), BF16), pltpu.VMEM((TM, D_RNN), BF16)]
        + [pltpu.VMEM((TM, LRU_BLOCK), F32)] * 6,
        compiler_params=_params(("arbitrary", "arbitrary")),
        name=f"lru{layer}",
    )(x, mod, mod, mod, ng.reshape(1, D_MODEL), w_in_r, conv_w, conv_b.reshape(1, D_RNN), w_gate_r,
      b_gate_r, lam, h0, w_out.astype(BF16))
    return out, st[:BATCH]


def _cmlp_kernel(x_ref, sh_ref, sc_ref, gt_ref, ng_ref, win_ref, bin_ref, vng_ref, ws_ref, bs_ref,
                 wout_ref, o_ref, u_scr, v_scr, uv_scr):
    gw = D_B // G_B
    h = _modulate(x_ref[...], ng_ref[...], sh_ref[...], sc_ref[...]).astype(BF16)
    ssq = jnp.zeros((TM_B, 1), F32)
    for g in range(G_B):
        cu = slice(g * gw, (g + 1) * gw)
        cv = slice(D_B + g * gw, D_B + (g + 1) * gw)
        u_scr[:, cu] = _gelu(_dot(h, win_ref[:, cu]) + bin_ref[:, cu])
        v = _gelu(_dot(h, win_ref[:, cv]) + bin_ref[:, cv])
        v_scr[:, cu] = v
        ssq = ssq + jnp.sum(v * v, axis=-1, keepdims=True)
    rinv = lax.rsqrt(ssq * (1.0 / D_B) + EPS)
    for g in range(G_B):
        cu = slice(g * gw, (g + 1) * gw)
        vn = ((v_scr[:, cu] * rinv) * vng_ref[:, cu]).astype(BF16)
        w_s = ws_ref[g]
        for c in range(TM_B // CHUNK):
            rows = slice(c * CHUNK, (c + 1) * CHUNK)
            sv = _dot(w_s, vn[rows, :]) + bs_ref[:, cu]
            uv_scr[rows, cu] = (u_scr[rows, cu] * sv).astype(BF16)
    o_ref[...] = x_ref[...] + gt_ref[...] * _dot(uv_scr[...], wout_ref[...])


def _cmlp_mixer(x, mod, layer, ng, w_in, b_in, vnorm_g, w_s, b_s, w_out):
    gw = D_B // G_B
    bs_full = jnp.repeat(b_s.T, gw, axis=1)
    const = lambda *shape: pl.BlockSpec(shape, lambda i: (0,) * len(shape), pipeline_mode=pl.Buffered(1))
    return pl.pallas_call(
        _cmlp_kernel,
        grid=(N_ROWS // TM_B,),
        in_specs=[
            pl.BlockSpec((TM_B, D_MODEL), lambda i: (i, 0)),
            _mod_spec(layer, 0, TM_B), _mod_spec(layer, 1, TM_B), _mod_spec(layer, 2, TM_B),
            const(1, D_MODEL),
            const(D_MODEL, 2 * D_B),
            const(1, 2 * D_B),
            const(1, D_B),
            const(G_B, CHUNK, CHUNK),
            const(CHUNK, D_B),
            const(D_B, D_MODEL),
        ],
        out_specs=pl.BlockSpec((TM_B, D_MODEL), lambda i: (i, 0)),
        out_shape=jax.ShapeDtypeStruct((N_ROWS, D_MODEL), F32),
        scratch_shapes=[pltpu.VMEM((TM_B, D_B), F32), pltpu.VMEM((TM_B, D_B), F32),
                        pltpu.VMEM((TM_B, D_B), BF16)],
        compiler_params=_params(("arbitrary",)),
        name=f"cmlp{layer}",
    )(x, mod, mod, mod, ng.reshape(1, D_MODEL), w_in.astype(BF16), b_in.reshape(1, 2 * D_B),
      vnorm_g.reshape(1, D_B), w_s.astype(BF16), bs_full, w_out.astype(BF16))


def _qkv_kernel(x_ref, sh_ref, sc_ref, ng_ref, w_ref, qkg_ref, gsum_ref, cos_ref, sin_ref,
                q_ref, kp_ref, ks_ref, vp_ref, vs_ref):
    i = pl.program_id(0)
    h = _modulate(x_ref[...], ng_ref[...], sh_ref[...], sc_ref[...]).astype(BF16)
    lane = lax.broadcasted_iota(jnp.int32, (TM_Q, LANES), 1)
    first_half = (lane & (HEAD_DIM_C - 1)) < HEAD_DIM_C // 2

    def norm_store(part, out_ref, rope):
        z = _dot(h, w_ref[:, part * D_MODEL:(part + 1) * D_MODEL])
        for hb in range(N_HEADS_C):
            zb = z[:, hb * LANES:(hb + 1) * LANES]
            sq = zb * zb
            hi = sq.astype(BF16)
            lo = (sq - hi.astype(F32)).astype(BF16)
            ssum = _dot(hi, gsum_ref[...]) + _dot(lo, gsum_ref[...])
            zb = zb * lax.rsqrt(ssum * (1.0 / HEAD_DIM_C) + EPS) * qkg_ref[part:part + 1, :]
            if rope:
                partner = jnp.where(first_half, pltpu.roll(zb, LANES - HEAD_DIM_C // 2, 1),
                                    pltpu.roll(zb, HEAD_DIM_C // 2, 1))
                zb = zb * cos_ref[...] + partner * sin_ref[...]
            out_ref[:, hb * LANES:(hb + 1) * LANES] = zb.astype(out_ref.dtype)

    def qkv(rope, k_ref, v_ref):
        norm_store(0, q_ref, rope)
        norm_store(1, k_ref, rope)
        v_ref[...] = _dot(h, w_ref[:, 2 * D_MODEL:])

    @pl.when(i < N_PROMPT // TM_Q)
    def _():
        qkv(False, kp_ref, vp_ref)

    @pl.when(i >= N_PROMPT // TM_Q)
    def _():
        qkv(True, ks_ref, vs_ref)


def _attn_heads(q_ref, k_ref, v_ref, lam_ref, sg_ref, o_ref, lam_init):
    lv = lam_ref[...]
    lam = (jnp.exp(jnp.sum(lv[0:1, :] * lv[1:2, :], axis=-1, keepdims=True))
           - jnp.exp(jnp.sum(lv[2:3, :] * lv[3:4, :], axis=-1, keepdims=True)) + lam_init)
    scale = HEAD_DIM_C ** -0.5
    lane = lax.broadcasted_iota(jnp.int32, (TQ, LANES), 1)
    comp0 = lane < HEAD_DIM_C
    for hd in range(N_HEADS_C):
        cols = slice(hd * LANES, (hd + 1) * LANES)
        qh = q_ref[:, cols]
        kh = k_ref[:, cols]
        vh = v_ref[:, cols]
        zero = jnp.zeros_like(qh)
        s0 = _dot_nt(jnp.where(comp0, qh, zero), kh) * scale
        s1 = _dot_nt(jnp.where(comp0, zero, qh), kh) * scale
        e0 = jnp.exp(s0 - jnp.max(s0, axis=-1, keepdims=True))
        e1 = jnp.exp(s1 - jnp.max(s1, axis=-1, keepdims=True))
        p0 = e0 / jnp.sum(e0, axis=-1, keepdims=True)
        p1 = e1 / jnp.sum(e1, axis=-1, keepdims=True)
        w = p0 - lam * p1
        o = _dot(w.astype(BF16), vh)
        o = o * lax.rsqrt(jnp.mean(o * o, axis=-1, keepdims=True) + EPS) * sg_ref[...]
        o_ref[:, cols] = (o * (1.0 - lam_init)).astype(BF16)


def _attn_prompt_kernel(q_ref, k_ref, v_ref, lam_ref, sg_ref, o_ref, k_scr, v_scr, *, lam_init):
    k_scr[...] = k_ref[...].astype(BF16)
    v_scr[...] = v_ref[...].astype(BF16)
    _attn_heads(q_ref, k_scr, v_scr, lam_ref, sg_ref, o_ref, lam_init)


def _attn_sample_kernel(q_ref, kl_ref, vl_ref, kc_ref, vc_ref, lam_ref, sg_ref, o_ref, k_scr, v_scr,
                        *, lam_init):
    @pl.when(pl.program_id(1) == 0)
    def _():
        k_scr[0:PAST_LEN, :] = kc_ref[...].astype(BF16)
        k_scr[PAST_LEN:, :] = kl_ref[...].astype(BF16)
        v_scr[0:PAST_LEN, :] = vc_ref[...].astype(BF16)
        v_scr[PAST_LEN:, :] = vl_ref[...].astype(BF16)

    _attn_heads(q_ref, k_scr, v_scr, lam_ref, sg_ref, o_ref, lam_init)


def _attn_out_kernel(x_ref, gt_ref, op_ref, os_ref, w_ref, o_ref):
    i = pl.program_id(0)
    w = w_ref[...].astype(BF16)

    @pl.when(i < N_PROMPT_TILES)
    def _():
        o_ref[...] = x_ref[...] + gt_ref[...] * _dot(op_ref[...], w)

    @pl.when(i >= N_PROMPT_TILES)
    def _():
        o_ref[...] = x_ref[...] + gt_ref[...] * _dot(os_ref[...], w)


def _rope_tables():
    t = jnp.arange(DEC_SEQ)
    inv = ROPE_BASE ** (-jnp.arange(N_FREQ_AXIS, dtype=F32) / N_FREQ_AXIS)
    ang = jnp.concatenate([(t // GRID_W)[:, None] * inv, (t % GRID_W)[:, None] * inv], axis=-1)
    cos, sin = jnp.cos(ang), jnp.sin(ang)
    cos64 = jnp.concatenate([cos, cos], axis=-1)
    sin64 = jnp.concatenate([-sin, sin], axis=-1)
    return jnp.tile(cos64, (1, 2)), jnp.tile(sin64, (1, 2))


def _attn_mixer(x, mod, layer, ng, cache_k, cache_v, w_qkv, qk_g, lam_vec, subln_g, w_out):
    lam_init = 0.8 - 0.6 * math.exp(-0.3 * layer)
    cos_t, sin_t = _rope_tables()
    qkg = jnp.tile(qk_g, (1, 2))
    lane = jnp.arange(LANES)
    gsum = (lane[:, None] // HEAD_DIM_C == lane[None, :] // HEAD_DIM_C).astype(BF16)
    n_prompt_q = N_PROMPT // TM_Q
    per_seq = DEC_SEQ // TM_Q
    prm = lambda i: (jnp.minimum(i, n_prompt_q - 1), 0)
    smp = lambda i: (jnp.maximum(i - n_prompt_q, 0), 0)
    rope_blk = lambda i: (i % per_seq, 0)
    q, kp, ks, vp, vs = pl.pallas_call(
        _qkv_kernel,
        grid=(N_ROWS // TM_Q,),
        in_specs=[
            pl.BlockSpec((TM_Q, D_MODEL), lambda i: (i, 0)),
            _mod_spec(layer, 0, TM_Q), _mod_spec(layer, 1, TM_Q),
            pl.BlockSpec((1, D_MODEL), lambda i: (0, 0)),
            pl.BlockSpec((D_MODEL, 3 * D_MODEL), lambda i: (0, 0), pipeline_mode=pl.Buffered(1)),
            pl.BlockSpec((2, LANES), lambda i: (0, 0)),
            pl.BlockSpec((LANES, LANES), lambda i: (0, 0)),
            pl.BlockSpec((TM_Q, LANES), rope_blk),
            pl.BlockSpec((TM_Q, LANES), rope_blk),
        ],
        out_specs=[
            pl.BlockSpec((TM_Q, D_MODEL), lambda i: (i, 0)),
            pl.BlockSpec((TM_Q, D_MODEL), prm), pl.BlockSpec((TM_Q, D_MODEL), smp),
            pl.BlockSpec((TM_Q, D_MODEL), prm), pl.BlockSpec((TM_Q, D_MODEL), smp),
        ],
        out_shape=[
            jax.ShapeDtypeStruct((N_ROWS, D_MODEL), BF16),
            jax.ShapeDtypeStruct((N_PROMPT, D_MODEL), F32), jax.ShapeDtypeStruct((N_SAMPLE, D_MODEL), F32),
            jax.ShapeDtypeStruct((N_PROMPT, D_MODEL), F32), jax.ShapeDtypeStruct((N_SAMPLE, D_MODEL), F32),
        ],
        compiler_params=_params(("arbitrary",)),
        name=f"qkv{layer}",
    )(x, mod, mod, ng.reshape(1, D_MODEL), w_qkv.astype(BF16), qkg, gsum, cos_t, sin_t)

    sg = subln_g.reshape(1, 2 * HEAD_DIM_C)
    o_p = pl.pallas_call(
        functools.partial(_attn_prompt_kernel, lam_init=lam_init),
        grid=(BATCH,),
        in_specs=[
            pl.BlockSpec((SEQ, D_MODEL), lambda b: (b, 0)),
            pl.BlockSpec((SEQ, D_MODEL), lambda b: (b, 0)),
            pl.BlockSpec((SEQ, D_MODEL), lambda b: (b, 0)),
            pl.BlockSpec((4, HEAD_DIM_C), lambda b: (0, 0)),
            pl.BlockSpec((1, 2 * HEAD_DIM_C), lambda b: (0, 0)),
        ],
        out_specs=pl.BlockSpec((SEQ, D_MODEL), lambda b: (b, 0)),
        out_shape=jax.ShapeDtypeStruct((N_PROMPT, D_MODEL), BF16),
        scratch_shapes=[pltpu.VMEM((SEQ, D_MODEL), BF16), pltpu.VMEM((SEQ, D_MODEL), BF16)],
        compiler_params=_params(("arbitrary",)),
        name=f"attn_prompt{layer}",
    )(q, kp, vp, lam_vec, sg)

    n_qb = DEC_SEQ // TQ
    t_all = PAST_LEN + DEC_SEQ
    o_s = pl.pallas_call(
        functools.partial(_attn_sample_kernel, lam_init=lam_init),
        grid=(DEC_BATCH, n_qb),
        in_specs=[
            pl.BlockSpec((TQ, D_MODEL), lambda b, t: (N_PROMPT // TQ + b * n_qb + t, 0)),
            pl.BlockSpec((DEC_SEQ, D_MODEL), lambda b, t: (b, 0)),
            pl.BlockSpec((DEC_SEQ, D_MODEL), lambda b, t: (b, 0)),
            pl.BlockSpec((None, PAST_LEN, D_MODEL), lambda b, t: (b, 0, 0)),
            pl.BlockSpec((None, PAST_LEN, D_MODEL), lambda b, t: (b, 0, 0)),
            pl.BlockSpec((4, HEAD_DIM_C), lambda b, t: (0, 0)),
            pl.BlockSpec((1, 2 * HEAD_DIM_C), lambda b, t: (0, 0)),
        ],
        out_specs=pl.BlockSpec((TQ, D_MODEL), lambda b, t: (b * n_qb + t, 0)),
        out_shape=jax.ShapeDtypeStruct((N_SAMPLE, D_MODEL), BF16),
        scratch_shapes=[pltpu.VMEM((t_all, D_MODEL), BF16), pltpu.VMEM((t_all, D_MODEL), BF16)],
        compiler_params=_params(("arbitrary", "arbitrary")),
        name=f"attn_sample{layer}",
    )(q, ks, vs, cache_k.reshape(DEC_BATCH, PAST_LEN, D_MODEL), cache_v.reshape(DEC_BATCH, PAST_LEN, D_MODEL),
      lam_vec, sg)

    x_new = pl.pallas_call(
        _attn_out_kernel,
        grid=(N_TILES,),
        in_specs=[
            pl.BlockSpec((TM, D_MODEL), lambda i: (i, 0)),
            _mod_spec(layer, 2),
            pl.BlockSpec((TM, D_MODEL), lambda i: (jnp.minimum(i, N_PROMPT_TILES - 1), 0)),
            pl.BlockSpec((TM, D_MODEL), lambda i: (jnp.maximum(i - N_PROMPT_TILES, 0), 0)),
            pl.BlockSpec((D_MODEL, D_MODEL), lambda i: (0, 0)),
        ],
        out_specs=pl.BlockSpec((TM, D_MODEL), lambda i: (i, 0)),
        out_shape=jax.ShapeDtypeStruct((N_ROWS, D_MODEL), F32),
        compiler_params=_params(("arbitrary",)),
        name=f"attn_out{layer}",
    )(x, mod, o_p, o_s, w_out)
    return x_new, kp, vp


def kernel(x_prompt, x_sample, state_lru, cache_k, cache_v, c, c_ctx, w_mod, b_mod, norm_g, lru_w_in, lru_conv_w, lru_conv_b, lru_w_gate, lru_b_gate, lru_lambda, lru_w_out, cmlp_w_in, cmlp_b_in, cmlp_norm_g, cmlp_w_s, cmlp_b_s, cmlp_w_out, attn_w_qkv, attn_qk_g, attn_lambda, attn_subln_g, attn_w_out, ffn_w_up, ffn_conv_w, ffn_conv_b, ffn_w_down):
    x = jnp.concatenate([x_prompt.reshape(N_PROMPT, D_MODEL), x_sample.reshape(N_SAMPLE, D_MODEL)], axis=0)
    cond8 = jnp.concatenate([c_ctx[None, :], c, jnp.zeros((SUBLANES - 1 - DEC_BATCH, D_MODEL), F32)], axis=0)
    mod = _adaln(cond8, w_mod, b_mod)
    new_lru, new_k, new_v = [], [], []
    for l in range(DEPTH):
        kind, j = l % 3, l // 3
        if kind == 0:
            x, st = _lru_mixer(x, mod, l, norm_g[l, 0], state_lru[:, j], lru_w_in[j], lru_conv_w[j],
                               lru_conv_b[j], lru_w_gate[j], lru_b_gate[j], lru_lambda[j], lru_w_out[j])
            new_lru.append(st)
        elif kind == 1:
            x = _cmlp_mixer(x, mod, l, norm_g[l, 0], cmlp_w_in[j], cmlp_b_in[j], cmlp_norm_g[j],
                            cmlp_w_s[j], cmlp_b_s[j], cmlp_w_out[j])
        else:
            x, kp, vp = _attn_mixer(x, mod, l, norm_g[l, 0], cache_k[:, j], cache_v[:, j], attn_w_qkv[j],
                                    attn_qk_g[j], attn_lambda[j], attn_subln_g[j], attn_w_out[j])
            new_k.append(kp.reshape(BATCH, SEQ, N_HEADS_C, 2, HEAD_DIM_C))
            new_v.append(vp.reshape(BATCH, SEQ, N_HEADS_C, 2 * HEAD_DIM_C))
        x = _ffn(x, mod, l, norm_g[l, 1], ffn_w_up[l], ffn_conv_w[l], ffn_conv_b[l], ffn_w_down[l])
    y_prompt = x[:N_PROMPT].reshape(BATCH, SEQ, D_MODEL)
    y_sample = x[N_PROMPT:].reshape(DEC_BATCH, DEC_SEQ, D_MODEL)
    return (y_prompt, y_sample, jnp.stack(new_lru, axis=1), jnp.stack(new_k, axis=1), jnp.stack(new_v, axis=1))
```

```python
import functools
import math

import jax
import jax.numpy as jnp
from jax import lax
from jax.experimental import pallas as pl
from jax.experimental.pallas import tpu as pltpu

D_MODEL = 1024
BATCH = 32
SEQ = 256
DEPTH = 4
DEC_BATCH = 2
DEC_SEQ = 1024
PAST_LEN = 512
GRID_W = 64
N_MOD = 6
EPS = 1e-6
D_RNN = 1280
LRU_BLOCKS = 10
LRU_BLOCK = 128
LRU_CONV_W = 4
LRU_CONV_LEFT = 2
LRU_C = 8.0
D_B = 2 * D_MODEL
CHUNK = 128
G_B = 8
N_HEADS_C = 8
HEAD_DIM_C = 64
N_FREQ_AXIS = 16
ROPE_BASE = 10000.0
D_FF = 2816
FFN_CONV_W = 3

N_PROMPT = BATCH * SEQ
N_SAMPLE = DEC_BATCH * DEC_SEQ
N_ROWS = N_PROMPT + N_SAMPLE
TM = 1024
N_TILES = N_ROWS // TM
N_PROMPT_TILES = N_PROMPT // TM
SEQ_PER_TILE = TM // SEQ
SUBLANES = 8
LANES = 128
N_SEG = SUBLANES
SEG = TM // N_SEG
FF_TN = 256
LRU_TN = 256
N_FF_CHUNKS = D_FF // FF_TN
TM_B = 512
TM_Q = 512
TQ = 256
VMEM_LIMIT = 56 * 1024 * 1024

BF16 = jnp.bfloat16
F32 = jnp.float32


def _dot(a, b):
    return jnp.dot(a, b, preferred_element_type=F32)


def _dot_nt(a, b):
    return lax.dot_general(a, b, (((1,), (1,)), ((), ())), preferred_element_type=F32)


def _mod_row(i):
    return jnp.maximum(i - (N_PROMPT_TILES - 1), 0)


def _modulate(x, ng, sh, sc):
    ms = jnp.mean(x * x, axis=-1, keepdims=True)
    return (x * lax.rsqrt(ms + EPS) * ng) * (1.0 + sc) + sh


def _gelu(x):
    k = math.sqrt(2.0 / math.pi)
    return 0.5 * x * (1.0 + jnp.tanh(k * (x + 0.044715 * (x * x * x))))


def _sigmoid(x):
    return 1.0 / (1.0 + jnp.exp(-x))


def _params(sem):
    return pltpu.CompilerParams(dimension_semantics=sem, vmem_limit_bytes=VMEM_LIMIT)


def _adaln_kernel(cond_ref, w_ref, b_ref, o_ref):
    cnd = cond_ref[...]
    s = (cnd * _sigmoid(cnd)).astype(BF16)
    o_ref[...] = _dot(s, w_ref[...].astype(BF16)) + b_ref[...]


def _adaln(cond8, w_mod, b_mod):
    out = pl.pallas_call(
        _adaln_kernel,
        grid=(DEPTH, N_MOD),
        in_specs=[
            pl.BlockSpec((SUBLANES, D_MODEL), lambda l, k: (0, 0)),
            pl.BlockSpec((None, D_MODEL, D_MODEL), lambda l, k: (l, 0, k)),
            pl.BlockSpec((None, None, 1, D_MODEL), lambda l, k: (l, k, 0, 0)),
        ],
        out_specs=pl.BlockSpec((None, None, SUBLANES, D_MODEL), lambda l, k: (l, k, 0, 0)),
        out_shape=jax.ShapeDtypeStruct((DEPTH, N_MOD, SUBLANES, D_MODEL), F32),
        compiler_params=_params(("arbitrary", "arbitrary")),
        name="adaln",
    )(cond8, w_mod, b_mod.reshape(DEPTH, N_MOD, 1, D_MODEL))
    return out.reshape(DEPTH, N_MOD, SUBLANES, 1, D_MODEL)


def _mod_spec(layer, k, tile_rows=TM):
    per = TM // tile_rows
    return pl.BlockSpec((None, None, None, 1, D_MODEL),
                        lambda i, *_: (layer, k, _mod_row(i // per), 0, 0))


def _ng_spec(layer, which):
    return pl.BlockSpec((None, None, 1, D_MODEL), lambda i, *_: (layer, which, 0, 0))


def _modulate_seg_major(x_ref, ng_ref, sh_ref, sc_ref, perm_scr, h_scr):
    for r in range(N_SEG):
        rows = slice(r * SEG, (r + 1) * SEG)
        h = _modulate(x_ref[rows, :], ng_ref[...], sh_ref[...], sc_ref[...])
        for cb in range(D_MODEL // LANES):
            perm_scr[cb, pl.ds(r, SEG, stride=N_SEG), :] = h[:, cb * LANES:(cb + 1) * LANES]
    for cb in range(D_MODEL // LANES):
        h_scr[:, cb * LANES:(cb + 1) * LANES] = perm_scr[cb].astype(BF16)


def _residual_from_seg_major(x_ref, gt_ref, perm_scr, o_ref):
    for r in range(N_SEG):
        rows = slice(r * SEG, (r + 1) * SEG)
        for cb in range(D_MODEL // LANES):
            cols = slice(cb * LANES, (cb + 1) * LANES)
            y = perm_scr[cb, pl.ds(r, SEG, stride=N_SEG), :]
            o_ref[rows, cols] = x_ref[rows, cols] + gt_ref[:, cols] * y


def _seg_masks(i, width):
    per_seq = jnp.where(i < N_PROMPT_TILES, SEQ // SEG, DEC_SEQ // SEG)
    r = lax.broadcasted_iota(jnp.int32, (N_SEG, width), 0) & (per_seq - 1)
    return r == 0, r == per_seq - 1


def _shift_prev(z, is_start):
    wrap = jnp.where(is_start, 0.0, pltpu.roll(z[TM - N_SEG:, :], 1, 0))
    return jnp.concatenate([wrap, z[:TM - N_SEG, :]], axis=0)


def _shift_next(z, is_end):
    wrap = jnp.where(is_end, 0.0, pltpu.roll(z[:N_SEG, :], N_SEG - 1, 0))
    return jnp.concatenate([z[N_SEG:, :], wrap], axis=0)


def _ffn_kernel(x_ref, sh_ref, sc_ref, gt_ref, ng_ref, wg_ref, wu_ref, cwg_ref, cwu_ref,
                cbg_ref, cbu_ref, wd_ref, o_ref, h_scr, acc_scr):
    i = pl.program_id(0)
    j = pl.program_id(1)

    @pl.when(j == 0)
    def _():
        _modulate_seg_major(x_ref, ng_ref, sh_ref, sc_ref, acc_scr, h_scr)
        acc_scr[...] = jnp.zeros_like(acc_scr)

    h = h_scr[...]
    is_start, is_end = _seg_masks(i, FF_TN)

    def branch(w_ref, cw_ref, cb_ref):
        z = _dot(h, w_ref[...].astype(BF16))
        return (cb_ref[...] + _shift_prev(z, is_start) * cw_ref[0:1, :] + z * cw_ref[1:2, :]
                + _shift_next(z, is_end) * cw_ref[2:3, :])

    g = branch(wg_ref, cwg_ref, cbg_ref)
    u = branch(wu_ref, cwu_ref, cbu_ref)
    act = (g * _sigmoid(g)) * u
    y = _dot(act.astype(BF16), wd_ref[...].astype(BF16))
    for cb in range(D_MODEL // LANES):
        acc_scr[cb] += y[:, cb * LANES:(cb + 1) * LANES]

    @pl.when(j == N_FF_CHUNKS - 1)
    def _():
        _residual_from_seg_major(x_ref, gt_ref, acc_scr, o_ref)


def _ffn(x, mod, layer, norm_g4, w_up, conv_w, conv_b, w_down):
    conv_b3 = conv_b.reshape(DEPTH, 1, 2 * D_FF)
    return pl.pallas_call(
        _ffn_kernel,
        grid=(N_TILES, N_FF_CHUNKS),
        in_specs=[
            pl.BlockSpec((TM, D_MODEL), lambda i, j: (i, 0)),
            _mod_spec(layer, 3), _mod_spec(layer, 4), _mod_spec(layer, 5),
            _ng_spec(layer, 1),
            pl.BlockSpec((None, D_MODEL, FF_TN), lambda i, j: (layer, 0, j)),
            pl.BlockSpec((None, D_MODEL, FF_TN), lambda i, j: (layer, 0, N_FF_CHUNKS + j)),
            pl.BlockSpec((None, FFN_CONV_W, FF_TN), lambda i, j: (layer, 0, j)),
            pl.BlockSpec((None, FFN_CONV_W, FF_TN), lambda i, j: (layer, 0, N_FF_CHUNKS + j)),
            pl.BlockSpec((None, 1, FF_TN), lambda i, j: (layer, 0, j)),
            pl.BlockSpec((None, 1, FF_TN), lambda i, j: (layer, 0, N_FF_CHUNKS + j)),
            pl.BlockSpec((None, FF_TN, D_MODEL), lambda i, j: (layer, j, 0)),
        ],
        out_specs=pl.BlockSpec((TM, D_MODEL), lambda i, j: (i, 0)),
        out_shape=jax.ShapeDtypeStruct((N_ROWS, D_MODEL), F32),
        scratch_shapes=[pltpu.VMEM((TM, D_MODEL), BF16), pltpu.VMEM((D_MODEL // LANES, TM, LANES), F32)],
        compiler_params=_params(("arbitrary", "arbitrary")),
        name=f"ffn{layer}",
    )(x, mod, mod, mod, norm_g4, w_up, w_up, conv_w, conv_w, conv_b3, conv_b3, w_down)


def _lru_kernel(x_ref, sh_ref, sc_ref, gt_ref, ng_ref, wig_ref, wir_ref, cw_ref, cb_ref, wgt_ref, bgt_ref,
                lam_ref, h0_ref, wout_ref, o_ref, st_ref,
                h_scr, yg_scr, perm_scr, gg_scr, af_scr, bf_scr, ab_scr, bb_scr):
    i = pl.program_id(0)
    n = pl.program_id(1)

    @pl.when(n == 0)
    def _():
        _modulate_seg_major(x_ref, ng_ref, sh_ref, sc_ref, perm_scr, h_scr)

    h = h_scr[...]
    is_start, is_end = _seg_masks(i, LRU_TN)
    gg_scr[...] = _gelu(_dot(h, wig_ref[...].astype(BF16)))
    rec = _dot(h, wir_ref[...].astype(BF16))
    prev1 = _shift_prev(rec, is_start)
    prev2 = _shift_prev(prev1, is_start)
    xc = (cb_ref[...] + prev2 * cw_ref[0:1, :] + prev1 * cw_ref[1:2, :] + rec * cw_ref[2:3, :]
          + _shift_next(rec, is_end) * cw_ref[3:4, :])

    lam = lam_ref[...]
    softplus = jnp.maximum(-lam, 0.0) + jnp.log1p(jnp.exp(-jnp.abs(lam)))
    for e in range(LRU_TN // LRU_BLOCK):
        cols = slice(e * LRU_BLOCK, (e + 1) * LRU_BLOCK)
        xce = xc[:, cols]
        w4 = jnp.concatenate([wgt_ref[d, g, e] for d in range(2) for g in range(2)], axis=1)
        b4 = jnp.concatenate([bgt_ref[d, g, e] for d in range(2) for g in range(2)], axis=1)
        gates = _sigmoid(_dot(xce.astype(BF16), w4.astype(BF16)) + b4)
        for d, (a_scr, b_scr) in enumerate(((af_scr, bf_scr), (ab_scr, bb_scr))):
            r = gates[:, (2 * d) * LRU_BLOCK:(2 * d + 1) * LRU_BLOCK]
            ig = gates[:, (2 * d + 1) * LRU_BLOCK:(2 * d + 2) * LRU_BLOCK]
            log_a = (-LRU_C * r) * softplus[d:d + 1, cols]
            a = jnp.exp(log_a)
            a_scr[:, cols] = a
            b_scr[:, cols] = jnp.sqrt(-jnp.tanh(log_a) * (a * a + 1.0)) * (ig * xce)

    def local_scan(t, carry):
        hf, cf, hb, cb = carry
        rf = pl.multiple_of(t * N_SEG, N_SEG)
        rb = pl.multiple_of((SEG - 1 - t) * N_SEG, N_SEG)
        a = af_scr[pl.ds(rf, N_SEG), :]
        hf = a * hf + bf_scr[pl.ds(rf, N_SEG), :]
        cf = a * cf
        bf_scr[pl.ds(rf, N_SEG), :] = hf
        af_scr[pl.ds(rf, N_SEG), :] = cf
        a = ab_scr[pl.ds(rb, N_SEG), :]
        hb = a * hb + bb_scr[pl.ds(rb, N_SEG), :]
        cb = a * cb
        bb_scr[pl.ds(rb, N_SEG), :] = hb
        ab_scr[pl.ds(rb, N_SEG), :] = cb
        return hf, cf, hb, cb

    zeros = jnp.zeros((N_SEG, LRU_TN), F32)
    ones = jnp.ones((N_SEG, LRU_TN), F32)
    end_f, dec_f, end_b, dec_b = lax.fori_loop(0, SEG, local_scan, (zeros, ones, zeros, ones), unroll=8)

    per_seq = jnp.where(i < N_PROMPT_TILES, SEQ // SEG, DEC_SEQ // SEG)
    sub = lax.broadcasted_iota(jnp.int32, (N_SEG, LRU_TN), 0)
    h0f = h0_ref[0:1, :]
    h0b = h0_ref[1:2, :]
    in_f, in_b = zeros, zeros
    out_f, out_b = [None] * N_SEG, [None] * N_SEG
    for r in range(N_SEG):
        hin = h0f if r == 0 else jnp.where((r & (per_seq - 1)) == 0, h0f, out_f[r - 1])
        in_f = jnp.where(sub == r, hin, in_f)
        out_f[r] = end_f[r:r + 1, :] + dec_f[r:r + 1, :] * hin
    for r in reversed(range(N_SEG)):
        hin = h0b if r == N_SEG - 1 else jnp.where((r & (per_seq - 1)) == per_seq - 1, h0b, out_b[r + 1])
        in_b = jnp.where(sub == r, hin, in_b)
        out_b[r] = end_b[r:r + 1, :] + dec_b[r:r + 1, :] * hin
    for q in range(SEQ_PER_TILE):
        st_ref[q, 0:1, :] = out_f[2 * q + 1]
        st_ref[q, 1:2, :] = out_b[2 * q]

    def every_step(v):
        return jnp.broadcast_to(v[None], (SEG, N_SEG, LRU_TN)).reshape(TM, LRU_TN)

    hsum = ((bf_scr[...] + af_scr[...] * every_step(in_f))
            + (bb_scr[...] + ab_scr[...] * every_step(in_b)))
    col = pl.multiple_of(n * LRU_TN, LRU_TN)
    yg_scr[:, pl.ds(col, LRU_TN)] = (gg_scr[...] * hsum).astype(BF16)

    @pl.when(n == D_RNN // LRU_TN - 1)
    def _():
        y = _dot(yg_scr[...], wout_ref[...].astype(BF16))
        for cb in range(D_MODEL // LANES):
            perm_scr[cb] = y[:, cb * LANES:(cb + 1) * LANES]
        _residual_from_seg_major(x_ref, gt_ref, perm_scr, o_ref)


def _lru_mixer(x, mod, layer, j, norm_g4, state_lru, w_in, conv_w, conv_b, w_gate, b_gate, lam, w_out):
    n_steps = D_RNN // LRU_TN
    per_step = LRU_TN // LRU_BLOCK
    n_lru = w_in.shape[0]
    h0 = jnp.concatenate([jnp.zeros((N_PROMPT_TILES, 2, D_RNN), F32), state_lru[:, j]], axis=0)
    out, st = pl.pallas_call(
        _lru_kernel,
        grid=(N_TILES, n_steps),
        in_specs=[
            pl.BlockSpec((TM, D_MODEL), lambda i, n: (i, 0)),
            _mod_spec(layer, 0), _mod_spec(layer, 1), _mod_spec(layer, 2),
            _ng_spec(layer, 0),
            pl.BlockSpec((None, D_MODEL, LRU_TN), lambda i, n: (j, 0, n)),
            pl.BlockSpec((None, D_MODEL, LRU_TN), lambda i, n: (j, 0, n_steps + n)),
            pl.BlockSpec((None, LRU_CONV_W, LRU_TN), lambda i, n: (j, 0, n)),
            pl.BlockSpec((None, 1, LRU_TN), lambda i, n: (j, 0, n)),
            pl.BlockSpec((None, 2, 2, per_step, LRU_BLOCK, LRU_BLOCK), lambda i, n: (j, 0, 0, n, 0, 0)),
            pl.BlockSpec((None, 2, 2, per_step, 1, LRU_BLOCK), lambda i, n: (j, 0, 0, n, 0, 0)),
            pl.BlockSpec((None, 2, LRU_TN), lambda i, n: (j, 0, n)),
            pl.BlockSpec((None, 2, LRU_TN), lambda i, n: (i, 0, n)),
            pl.BlockSpec((None, D_RNN, D_MODEL), lambda i, n: (j, 0, 0), pipeline_mode=pl.Buffered(1)),
        ],
        out_specs=[
            pl.BlockSpec((TM, D_MODEL), lambda i, n: (i, 0)),
            pl.BlockSpec((SEQ_PER_TILE, 2, LRU_TN), lambda i, n: (i, 0, n)),
        ],
        out_shape=[
            jax.ShapeDtypeStruct((N_ROWS, D_MODEL), F32),
            jax.ShapeDtypeStruct((N_TILES * SEQ_PER_TILE, 2, D_RNN), F32),
        ],
        scratch_shapes=[pltpu.VMEM((TM, D_MODEL), BF16), pltpu.VMEM((TM, D_RNN), BF16),
                        pltpu.VMEM((D_MODEL // LANES, TM, LANES), F32)]
        + [pltpu.VMEM((TM, LRU_TN), F32)] * 5,
        compiler_params=_params(("arbitrary", "arbitrary")),
        name=f"lru{layer}",
    )(x, mod, mod, mod, norm_g4, w_in, w_in, conv_w, conv_b.reshape(n_lru, 1, D_RNN), w_gate,
      b_gate.reshape(n_lru, 2, 2, LRU_BLOCKS, 1, LRU_BLOCK), lam, h0, w_out)
    return out, st[:BATCH]


def _cmlp_kernel(x_ref, sh_ref, sc_ref, gt_ref, ng_ref, win_ref, bin_ref, vng_ref, ws_ref, bs_ref,
                 wout_ref, o_ref, u_scr, v_scr, uv_scr):
    gw = D_B // G_B
    h = _modulate(x_ref[...], ng_ref[...], sh_ref[...], sc_ref[...]).astype(BF16)
    ssq = jnp.zeros((TM_B, 1), F32)
    for g in range(G_B):
        cu = slice(g * gw, (g + 1) * gw)
        cv = slice(D_B + g * gw, D_B + (g + 1) * gw)
        u_scr[:, cu] = _gelu(_dot(h, win_ref[:, cu]) + bin_ref[:, cu])
        v = _gelu(_dot(h, win_ref[:, cv]) + bin_ref[:, cv])
        v_scr[:, cu] = v
        ssq = ssq + jnp.sum(v * v, axis=-1, keepdims=True)
    rinv = lax.rsqrt(ssq * (1.0 / D_B) + EPS)
    for g in range(G_B):
        cu = slice(g * gw, (g + 1) * gw)
        vn = ((v_scr[:, cu] * rinv) * vng_ref[:, cu]).astype(BF16)
        w_s = ws_ref[g]
        for c in range(TM_B // CHUNK):
            rows = slice(c * CHUNK, (c + 1) * CHUNK)
            sv = _dot(w_s, vn[rows, :]) + bs_ref[:, cu]
            uv_scr[rows, cu] = (u_scr[rows, cu] * sv).astype(BF16)
    o_ref[...] = x_ref[...] + gt_ref[...] * _dot(uv_scr[...], wout_ref[...])


def _cmlp_mixer(x, mod, layer, ng, w_in, b_in, vnorm_g, w_s, b_s, w_out):
    gw = D_B // G_B
    bs_full = jnp.repeat(b_s.T, gw, axis=1)
    const = lambda *shape: pl.BlockSpec(shape, lambda i: (0,) * len(shape), pipeline_mode=pl.Buffered(1))
    return pl.pallas_call(
        _cmlp_kernel,
        grid=(N_ROWS // TM_B,),
        in_specs=[
            pl.BlockSpec((TM_B, D_MODEL), lambda i: (i, 0)),
            _mod_spec(layer, 0, TM_B), _mod_spec(layer, 1, TM_B), _mod_spec(layer, 2, TM_B),
            const(1, D_MODEL),
            const(D_MODEL, 2 * D_B),
            const(1, 2 * D_B),
            const(1, D_B),
            const(G_B, CHUNK, CHUNK),
            const(CHUNK, D_B),
            const(D_B, D_MODEL),
        ],
        out_specs=pl.BlockSpec((TM_B, D_MODEL), lambda i: (i, 0)),
        out_shape=jax.ShapeDtypeStruct((N_ROWS, D_MODEL), F32),
        scratch_shapes=[pltpu.VMEM((TM_B, D_B), F32), pltpu.VMEM((TM_B, D_B), F32),
                        pltpu.VMEM((TM_B, D_B), BF16)],
        compiler_params=_params(("arbitrary",)),
        name=f"cmlp{layer}",
    )(x, mod, mod, mod, ng.reshape(1, D_MODEL), w_in.astype(BF16), b_in.reshape(1, 2 * D_B),
      vnorm_g.reshape(1, D_B), w_s.astype(BF16), bs_full, w_out.astype(BF16))


def _qkv_kernel(x_ref, sh_ref, sc_ref, ng_ref, w_ref, qkg_ref, gsum_ref, cos_ref, sin_ref,
                q_ref, kp_ref, ks_ref, vp_ref, vs_ref):
    i = pl.program_id(0)
    h = _modulate(x_ref[...], ng_ref[...], sh_ref[...], sc_ref[...]).astype(BF16)
    lane = lax.broadcasted_iota(jnp.int32, (TM_Q, LANES), 1)
    first_half = (lane & (HEAD_DIM_C - 1)) < HEAD_DIM_C // 2

    def norm_store(part, out_ref, rope):
        z = _dot(h, w_ref[:, part * D_MODEL:(part + 1) * D_MODEL])
        for hb in range(N_HEADS_C):
            zb = z[:, hb * LANES:(hb + 1) * LANES]
            sq = zb * zb
            hi = sq.astype(BF16)
            lo = (sq - hi.astype(F32)).astype(BF16)
            ssum = _dot(hi, gsum_ref[...]) + _dot(lo, gsum_ref[...])
            zb = zb * lax.rsqrt(ssum * (1.0 / HEAD_DIM_C) + EPS) * qkg_ref[part:part + 1, :]
            if rope:
                partner = jnp.where(first_half, pltpu.roll(zb, LANES - HEAD_DIM_C // 2, 1),
                                    pltpu.roll(zb, HEAD_DIM_C // 2, 1))
                zb = zb * cos_ref[...] + partner * sin_ref[...]
            out_ref[:, hb * LANES:(hb + 1) * LANES] = zb.astype(out_ref.dtype)

    def qkv(rope, k_ref, v_ref):
        norm_store(0, q_ref, rope)
        norm_store(1, k_ref, rope)
        v_ref[...] = _dot(h, w_ref[:, 2 * D_MODEL:])

    @pl.when(i < N_PROMPT // TM_Q)
    def _():
        qkv(False, kp_ref, vp_ref)

    @pl.when(i >= N_PROMPT // TM_Q)
    def _():
        qkv(True, ks_ref, vs_ref)


def _attn_heads(q_ref, k_ref, v_ref, lam_ref, sg_ref, o_ref, lam_init):
    lv = lam_ref[...]
    lam = (jnp.exp(jnp.sum(lv[0:1, :] * lv[1:2, :], axis=-1, keepdims=True))
           - jnp.exp(jnp.sum(lv[2:3, :] * lv[3:4, :], axis=-1, keepdims=True)) + lam_init)
    scale = HEAD_DIM_C ** -0.5
    lane = lax.broadcasted_iota(jnp.int32, (TQ, LANES), 1)
    comp0 = lane < HEAD_DIM_C
    for hd in range(N_HEADS_C):
        cols = slice(hd * LANES, (hd + 1) * LANES)
        qh = q_ref[:, cols]
        kh = k_ref[:, cols]
        vh = v_ref[:, cols]
        zero = jnp.zeros_like(qh)
        s0 = _dot_nt(jnp.where(comp0, qh, zero), kh) * scale
        s1 = _dot_nt(jnp.where(comp0, zero, qh), kh) * scale
        e0 = jnp.exp(s0 - jnp.max(s0, axis=-1, keepdims=True))
        e1 = jnp.exp(s1 - jnp.max(s1, axis=-1, keepdims=True))
        p0 = e0 / jnp.sum(e0, axis=-1, keepdims=True)
        p1 = e1 / jnp.sum(e1, axis=-1, keepdims=True)
        w = p0 - lam * p1
        o = _dot(w.astype(BF16), vh)
        o = o * lax.rsqrt(jnp.mean(o * o, axis=-1, keepdims=True) + EPS) * sg_ref[...]
        o_ref[:, cols] = (o * (1.0 - lam_init)).astype(BF16)


def _attn_prompt_kernel(q_ref, k_ref, v_ref, lam_ref, sg_ref, o_ref, k_scr, v_scr, *, lam_init):
    k_scr[...] = k_ref[...].astype(BF16)
    v_scr[...] = v_ref[...].astype(BF16)
    _attn_heads(q_ref, k_scr, v_scr, lam_ref, sg_ref, o_ref, lam_init)


def _attn_sample_kernel(q_ref, kl_ref, vl_ref, kc_ref, vc_ref, lam_ref, sg_ref, o_ref, k_scr, v_scr,
                        *, lam_init):
    @pl.when(pl.program_id(1) == 0)
    def _():
        k_scr[0:PAST_LEN, :] = kc_ref[...].astype(BF16)
        k_scr[PAST_LEN:, :] = kl_ref[...].astype(BF16)
        v_scr[0:PAST_LEN, :] = vc_ref[...].astype(BF16)
        v_scr[PAST_LEN:, :] = vl_ref[...].astype(BF16)

    _attn_heads(q_ref, k_scr, v_scr, lam_ref, sg_ref, o_ref, lam_init)


def _attn_out_kernel(x_ref, gt_ref, op_ref, os_ref, w_ref, o_ref):
    i = pl.program_id(0)
    w = w_ref[...].astype(BF16)

    @pl.when(i < N_PROMPT_TILES)
    def _():
        o_ref[...] = x_ref[...] + gt_ref[...] * _dot(op_ref[...], w)

    @pl.when(i >= N_PROMPT_TILES)
    def _():
        o_ref[...] = x_ref[...] + gt_ref[...] * _dot(os_ref[...], w)


def _rope_tables():
    t = jnp.arange(DEC_SEQ)
    inv = ROPE_BASE ** (-jnp.arange(N_FREQ_AXIS, dtype=F32) / N_FREQ_AXIS)
    ang = jnp.concatenate([(t // GRID_W)[:, None] * inv, (t % GRID_W)[:, None] * inv], axis=-1)
    cos, sin = jnp.cos(ang), jnp.sin(ang)
    cos64 = jnp.concatenate([cos, cos], axis=-1)
    sin64 = jnp.concatenate([-sin, sin], axis=-1)
    return jnp.tile(cos64, (1, 2)), jnp.tile(sin64, (1, 2))


def _attn_mixer(x, mod, layer, ng, cache_k, cache_v, w_qkv, qk_g, lam_vec, subln_g, w_out):
    lam_init = 0.8 - 0.6 * math.exp(-0.3 * layer)
    cos_t, sin_t = _rope_tables()
    qkg = jnp.tile(qk_g, (1, 2))
    lane = jnp.arange(LANES)
    gsum = (lane[:, None] // HEAD_DIM_C == lane[None, :] // HEAD_DIM_C).astype(BF16)
    n_prompt_q = N_PROMPT // TM_Q
    per_seq = DEC_SEQ // TM_Q
    prm = lambda i: (jnp.minimum(i, n_prompt_q - 1), 0)
    smp = lambda i: (jnp.maximum(i - n_prompt_q, 0), 0)
    rope_blk = lambda i: (i % per_seq, 0)
    q, kp, ks, vp, vs = pl.pallas_call(
        _qkv_kernel,
        grid=(N_ROWS // TM_Q,),
        in_specs=[
            pl.BlockSpec((TM_Q, D_MODEL), lambda i: (i, 0)),
            _mod_spec(layer, 0, TM_Q), _mod_spec(layer, 1, TM_Q),
            pl.BlockSpec((1, D_MODEL), lambda i: (0, 0)),
            pl.BlockSpec((D_MODEL, 3 * D_MODEL), lambda i: (0, 0), pipeline_mode=pl.Buffered(1)),
            pl.BlockSpec((2, LANES), lambda i: (0, 0)),
            pl.BlockSpec((LANES, LANES), lambda i: (0, 0)),
            pl.BlockSpec((TM_Q, LANES), rope_blk),
            pl.BlockSpec((TM_Q, LANES), rope_blk),
        ],
        out_specs=[
            pl.BlockSpec((TM_Q, D_MODEL), lambda i: (i, 0)),
            pl.BlockSpec((TM_Q, D_MODEL), prm), pl.BlockSpec((TM_Q, D_MODEL), smp),
            pl.BlockSpec((TM_Q, D_MODEL), prm), pl.BlockSpec((TM_Q, D_MODEL), smp),
        ],
        out_shape=[
            jax.ShapeDtypeStruct((N_ROWS, D_MODEL), BF16),
            jax.ShapeDtypeStruct((N_PROMPT, D_MODEL), F32), jax.ShapeDtypeStruct((N_SAMPLE, D_MODEL), F32),
            jax.ShapeDtypeStruct((N_PROMPT, D_MODEL), F32), jax.ShapeDtypeStruct((N_SAMPLE, D_MODEL), F32),
        ],
        compiler_params=_params(("arbitrary",)),
        name=f"qkv{layer}",
    )(x, mod, mod, ng.reshape(1, D_MODEL), w_qkv.astype(BF16), qkg, gsum, cos_t, sin_t)

    sg = subln_g.reshape(1, 2 * HEAD_DIM_C)
    o_p = pl.pallas_call(
        functools.partial(_attn_prompt_kernel, lam_init=lam_init),
        grid=(BATCH,),
        in_specs=[
            pl.BlockSpec((SEQ, D_MODEL), lambda b: (b, 0)),
            pl.BlockSpec((SEQ, D_MODEL), lambda b: (b, 0)),
            pl.BlockSpec((SEQ, D_MODEL), lambda b: (b, 0)),
            pl.BlockSpec((4, HEAD_DIM_C), lambda b: (0, 0)),
            pl.BlockSpec((1, 2 * HEAD_DIM_C), lambda b: (0, 0)),
        ],
        out_specs=pl.BlockSpec((SEQ, D_MODEL), lambda b: (b, 0)),
        out_shape=jax.ShapeDtypeStruct((N_PROMPT, D_MODEL), BF16),
        scratch_shapes=[pltpu.VMEM((SEQ, D_MODEL), BF16), pltpu.VMEM((SEQ, D_MODEL), BF16)],
        compiler_params=_params(("arbitrary",)),
        name=f"attn_prompt{layer}",
    )(q, kp, vp, lam_vec, sg)

    n_qb = DEC_SEQ // TQ
    t_all = PAST_LEN + DEC_SEQ
    o_s = pl.pallas_call(
        functools.partial(_attn_sample_kernel, lam_init=lam_init),
        grid=(DEC_BATCH, n_qb),
        in_specs=[
            pl.BlockSpec((TQ, D_MODEL), lambda b, t: (N_PROMPT // TQ + b * n_qb + t, 0)),
            pl.BlockSpec((DEC_SEQ, D_MODEL), lambda b, t: (b, 0)),
            pl.BlockSpec((DEC_SEQ, D_MODEL), lambda b, t: (b, 0)),
            pl.BlockSpec((None, PAST_LEN, D_MODEL), lambda b, t: (b, 0, 0)),
            pl.BlockSpec((None, PAST_LEN, D_MODEL), lambda b, t: (b, 0, 0)),
            pl.BlockSpec((4, HEAD_DIM_C), lambda b, t: (0, 0)),
            pl.BlockSpec((1, 2 * HEAD_DIM_C), lambda b, t: (0, 0)),
        ],
        out_specs=pl.BlockSpec((TQ, D_MODEL), lambda b, t: (b * n_qb + t, 0)),
        out_shape=jax.ShapeDtypeStruct((N_SAMPLE, D_MODEL), BF16),
        scratch_shapes=[pltpu.VMEM((t_all, D_MODEL), BF16), pltpu.VMEM((t_all, D_MODEL), BF16)],
        compiler_params=_params(("arbitrary", "arbitrary")),
        name=f"attn_sample{layer}",
    )(q, ks, vs, cache_k.reshape(DEC_BATCH, PAST_LEN, D_MODEL), cache_v.reshape(DEC_BATCH, PAST_LEN, D_MODEL),
      lam_vec, sg)

    x_new = pl.pallas_call(
        _attn_out_kernel,
        grid=(N_TILES,),
        in_specs=[
            pl.BlockSpec((TM, D_MODEL), lambda i: (i, 0)),
            _mod_spec(layer, 2),
            pl.BlockSpec((TM, D_MODEL), lambda i: (jnp.minimum(i, N_PROMPT_TILES - 1), 0)),
            pl.BlockSpec((TM, D_MODEL), lambda i: (jnp.maximum(i - N_PROMPT_TILES, 0), 0)),
            pl.BlockSpec((D_MODEL, D_MODEL), lambda i: (0, 0)),
        ],
        out_specs=pl.BlockSpec((TM, D_MODEL), lambda i: (i, 0)),
        out_shape=jax.ShapeDtypeStruct((N_ROWS, D_MODEL), F32),
        compiler_params=_params(("arbitrary",)),
        name=f"attn_out{layer}",
    )(x, mod, o_p, o_s, w_out)
    return x_new, kp, vp


def kernel(x_prompt, x_sample, state_lru, cache_k, cache_v, c, c_ctx, w_mod, b_mod, norm_g, lru_w_in, lru_conv_w, lru_conv_b, lru_w_gate, lru_b_gate, lru_lambda, lru_w_out, cmlp_w_in, cmlp_b_in, cmlp_norm_g, cmlp_w_s, cmlp_b_s, cmlp_w_out, attn_w_qkv, attn_qk_g, attn_lambda, attn_subln_g, attn_w_out, ffn_w_up, ffn_conv_w, ffn_conv_b, ffn_w_down):
    x = jnp.concatenate([x_prompt.reshape(N_PROMPT, D_MODEL), x_sample.reshape(N_SAMPLE, D_MODEL)], axis=0)
    cond8 = jnp.concatenate([c_ctx[None, :], c, jnp.zeros((SUBLANES - 1 - DEC_BATCH, D_MODEL), F32)], axis=0)
    mod = _adaln(cond8, w_mod, b_mod)
    norm_g4 = norm_g.reshape(DEPTH, 2, 1, D_MODEL)
    new_lru, new_k, new_v = [], [], []
    for l in range(DEPTH):
        kind, j = l % 3, l // 3
        if kind == 0:
            x, st = _lru_mixer(x, mod, l, j, norm_g4, state_lru, lru_w_in, lru_conv_w, lru_conv_b,
                               lru_w_gate, lru_b_gate, lru_lambda, lru_w_out)
            new_lru.append(st)
        elif kind == 1:
            x = _cmlp_mixer(x, mod, l, norm_g[l, 0], cmlp_w_in[j], cmlp_b_in[j], cmlp_norm_g[j],
                            cmlp_w_s[j], cmlp_b_s[j], cmlp_w_out[j])
        else:
            x, kp, vp = _attn_mixer(x, mod, l, norm_g[l, 0], cache_k[:, j], cache_v[:, j], attn_w_qkv[j],
                                    attn_qk_g[j], attn_lambda[j], attn_subln_g[j], attn_w_out[j])
            new_k.append(kp.reshape(BATCH, SEQ, N_HEADS_C, 2, HEAD_DIM_C))
            new_v.append(vp.reshape(BATCH, SEQ, N_HEADS_C, 2 * HEAD_DIM_C))
        x = _ffn(x, mod, l, norm_g4, ffn_w_up, ffn_conv_w, ffn_conv_b, ffn_w_down)
    y_prompt = x[:N_PROMPT].reshape(BATCH, SEQ, D_MODEL)
    y_sample = x[N_PROMPT:].reshape(DEC_BATCH, DEC_SEQ, D_MODEL)
    return (y_prompt, y_sample, jnp.stack(new_lru, axis=1), jnp.stack(new_k, axis=1), jnp.stack(new_v, axis=1))
```

```python
import functools
import math

import jax
import jax.numpy as jnp
from jax import lax
from jax.experimental import pallas as pl
from jax.experimental.pallas import tpu as pltpu

D_MODEL = 1024
BATCH = 32
SEQ = 256
DEPTH = 4
DEC_BATCH = 2
DEC_SEQ = 1024
PAST_LEN = 512
GRID_W = 64
N_MOD = 6
EPS = 1e-6
D_RNN = 1280
LRU_BLOCKS = 10
LRU_BLOCK = 128
LRU_CONV_W = 4
LRU_CONV_LEFT = 2
LRU_C = 8.0
D_B = 2 * D_MODEL
CHUNK = 128
G_B = 8
N_HEADS_C = 8
HEAD_DIM_C = 64
N_FREQ_AXIS = 16
ROPE_BASE = 10000.0
D_FF = 2816
FFN_CONV_W = 3

N_PROMPT = BATCH * SEQ
N_SAMPLE = DEC_BATCH * DEC_SEQ
N_ROWS = N_PROMPT + N_SAMPLE
TM = 1024
N_TILES = N_ROWS // TM
N_PROMPT_TILES = N_PROMPT // TM
SEQ_PER_TILE = TM // SEQ
SUBLANES = 8
LANES = 128
N_SEG = SUBLANES
SEG = TM // N_SEG
FF_TN = 256
FF_BLK = 128
LRU_TN = 256
N_FF_CHUNKS = D_FF // FF_TN
TM_B = 512
TM_Q = 512
TQ = 256
VMEM_LIMIT = 56 * 1024 * 1024

BF16 = jnp.bfloat16
F32 = jnp.float32


def _dot(a, b):
    return jnp.dot(a, b, preferred_element_type=F32)


def _dot_nt(a, b):
    return lax.dot_general(a, b, (((1,), (1,)), ((), ())), preferred_element_type=F32)


def _mod_row(i):
    return jnp.maximum(i - (N_PROMPT_TILES - 1), 0)


def _modulate(x, ng, sh, sc):
    ms = jnp.mean(x * x, axis=-1, keepdims=True)
    return (x * lax.rsqrt(ms + EPS) * ng) * (1.0 + sc) + sh


def _gelu(x):
    k = math.sqrt(2.0 / math.pi)
    return (0.5 * x) * (1.0 + jnp.tanh(x * (k + (k * 0.044715) * (x * x))))


def _sigmoid(x):
    return 0.5 * jnp.tanh(0.5 * x) + 0.5


def _params(sem):
    return pltpu.CompilerParams(dimension_semantics=sem, vmem_limit_bytes=VMEM_LIMIT)


def _adaln_kernel(cond_ref, w_ref, b_ref, o_ref):
    cnd = cond_ref[...]
    s = (cnd * _sigmoid(cnd)).astype(BF16)
    o_ref[...] = _dot(s, w_ref[...].astype(BF16)) + b_ref[...]


def _adaln(cond8, w_mod, b_mod):
    out = pl.pallas_call(
        _adaln_kernel,
        grid=(DEPTH, N_MOD),
        in_specs=[
            pl.BlockSpec((SUBLANES, D_MODEL), lambda l, k: (0, 0)),
            pl.BlockSpec((None, D_MODEL, D_MODEL), lambda l, k: (l, 0, k)),
            pl.BlockSpec((None, None, 1, D_MODEL), lambda l, k: (l, k, 0, 0)),
        ],
        out_specs=pl.BlockSpec((None, None, SUBLANES, D_MODEL), lambda l, k: (l, k, 0, 0)),
        out_shape=jax.ShapeDtypeStruct((DEPTH, N_MOD, SUBLANES, D_MODEL), F32),
        compiler_params=_params(("arbitrary", "arbitrary")),
        name="adaln",
    )(cond8, w_mod, b_mod.reshape(DEPTH, N_MOD, 1, D_MODEL))
    return out.reshape(DEPTH, N_MOD, SUBLANES, 1, D_MODEL)


def _mod_spec(layer, k, tile_rows=TM):
    per = TM // tile_rows
    return pl.BlockSpec((None, None, None, 1, D_MODEL),
                        lambda i, *_: (layer, k, _mod_row(i // per), 0, 0))


def _ng_spec(layer, which):
    return pl.BlockSpec((None, None, 1, D_MODEL), lambda i, *_: (layer, which, 0, 0))


def _modulate_seg_major(x_ref, ng_ref, sh_ref, sc_ref, perm_scr, h_scr):
    for r in range(N_SEG):
        rows = slice(r * SEG, (r + 1) * SEG)
        h = _modulate(x_ref[rows, :], ng_ref[...], sh_ref[...], sc_ref[...])
        for cb in range(D_MODEL // LANES):
            perm_scr[cb, pl.ds(r, SEG, stride=N_SEG), :] = h[:, cb * LANES:(cb + 1) * LANES]
    for cb in range(D_MODEL // LANES):
        h_scr[:, cb * LANES:(cb + 1) * LANES] = perm_scr[cb].astype(BF16)


def _residual_from_seg_major(x_ref, gt_ref, perm_scr, o_ref):
    for r in range(N_SEG):
        rows = slice(r * SEG, (r + 1) * SEG)
        for cb in range(D_MODEL // LANES):
            cols = slice(cb * LANES, (cb + 1) * LANES)
            y = perm_scr[cb, pl.ds(r, SEG, stride=N_SEG), :]
            o_ref[rows, cols] = x_ref[rows, cols] + gt_ref[:, cols] * y


def _seg_masks(i, width):
    per_seq = jnp.where(i < N_PROMPT_TILES, SEQ // SEG, DEC_SEQ // SEG)
    r = lax.broadcasted_iota(jnp.int32, (N_SEG, width), 0) & (per_seq - 1)
    return r == 0, r == per_seq - 1


def _shift_prev(z, is_start):
    wrap = jnp.where(is_start, 0.0, pltpu.roll(z[TM - N_SEG:, :], 1, 0))
    return jnp.concatenate([wrap, z[:TM - N_SEG, :]], axis=0)


def _shift_next(z, is_end):
    wrap = jnp.where(is_end, 0.0, pltpu.roll(z[:N_SEG, :], N_SEG - 1, 0))
    return jnp.concatenate([z[N_SEG:, :], wrap], axis=0)


def _ffn_kernel(x_ref, sh_ref, sc_ref, gt_ref, ng_ref, wg_ref, wu_ref, cwg_ref, cwu_ref,
                cbg_ref, cbu_ref, wd_ref, o_ref, h_scr, acc_scr, w_scr, z_scr, act_scr):
    i = pl.program_id(0)
    j = pl.program_id(1)

    @pl.when(j == 0)
    def _():
        _modulate_seg_major(x_ref, ng_ref, sh_ref, sc_ref, acc_scr, h_scr)
        acc_scr[...] = jnp.zeros_like(acc_scr)

    is_start, is_end = _seg_masks(i, 2 * FF_TN)
    w_scr[:, :FF_TN] = wg_ref[...].astype(BF16)
    w_scr[:, FF_TN:] = wu_ref[...].astype(BF16)
    cw = jnp.concatenate([cwg_ref[...], cwu_ref[...]], axis=1)
    cb = jnp.concatenate([cbg_ref[...], cbu_ref[...]], axis=1)

    def up(m):
        r0 = m * FF_BLK
        z_scr[N_SEG + r0:N_SEG + r0 + FF_BLK, :] = _dot(h_scr[r0:r0 + FF_BLK, :], w_scr[...])

    def post(m):
        r0 = N_SEG + m * FF_BLK
        c = (cb + z_scr[r0 - N_SEG:r0 - N_SEG + FF_BLK, :] * cw[0:1, :] + z_scr[r0:r0 + FF_BLK, :] * cw[1:2, :]
             + z_scr[r0 + N_SEG:r0 + N_SEG + FF_BLK, :] * cw[2:3, :])
        hg = 0.5 * c[:, :FF_TN]
        act = (hg * (jnp.tanh(hg) + 1.0)) * c[:, FF_TN:]
        act_scr[m * FF_BLK:(m + 1) * FF_BLK, :] = act.astype(BF16)

    n_blk = TM // FF_BLK
    up(n_blk - 1)
    z_scr[0:N_SEG, :] = jnp.where(is_start, 0.0, pltpu.roll(z_scr[TM:TM + N_SEG, :], 1, 0))
    up(0)
    z_scr[TM + N_SEG:, :] = jnp.where(is_end, 0.0, pltpu.roll(z_scr[N_SEG:2 * N_SEG, :], N_SEG - 1, 0))
    up(1)
    for m in range(n_blk):
        if m + 2 < n_blk - 1:
            up(m + 2)
        post(m)
    y = _dot(act_scr[...], wd_ref[...].astype(BF16))
    for cb_i in range(D_MODEL // LANES):
        acc_scr[cb_i] += y[:, cb_i * LANES:(cb_i + 1) * LANES]

    @pl.when(j == N_FF_CHUNKS - 1)
    def _():
        _residual_from_seg_major(x_ref, gt_ref, acc_scr, o_ref)


def _ffn(x, mod, layer, norm_g4, w_up, conv_w, conv_b, w_down):
    conv_b3 = conv_b.reshape(DEPTH, 1, 2 * D_FF)
    return pl.pallas_call(
        _ffn_kernel,
        grid=(N_TILES, N_FF_CHUNKS),
        in_specs=[
            pl.BlockSpec((TM, D_MODEL), lambda i, j: (i, 0)),
            _mod_spec(layer, 3), _mod_spec(layer, 4), _mod_spec(layer, 5),
            _ng_spec(layer, 1),
            pl.BlockSpec((None, D_MODEL, FF_TN), lambda i, j: (layer, 0, j)),
            pl.BlockSpec((None, D_MODEL, FF_TN), lambda i, j: (layer, 0, N_FF_CHUNKS + j)),
            pl.BlockSpec((None, FFN_CONV_W, FF_TN), lambda i, j: (layer, 0, j)),
            pl.BlockSpec((None, FFN_CONV_W, FF_TN), lambda i, j: (layer, 0, N_FF_CHUNKS + j)),
            pl.BlockSpec((None, 1, FF_TN), lambda i, j: (layer, 0, j)),
            pl.BlockSpec((None, 1, FF_TN), lambda i, j: (layer, 0, N_FF_CHUNKS + j)),
            pl.BlockSpec((None, FF_TN, D_MODEL), lambda i, j: (layer, j, 0)),
        ],
        out_specs=pl.BlockSpec((TM, D_MODEL), lambda i, j: (i, 0)),
        out_shape=jax.ShapeDtypeStruct((N_ROWS, D_MODEL), F32),
        scratch_shapes=[pltpu.VMEM((TM, D_MODEL), BF16), pltpu.VMEM((D_MODEL // LANES, TM, LANES), F32),
                        pltpu.VMEM((D_MODEL, 2 * FF_TN), BF16), pltpu.VMEM((TM + 2 * N_SEG, 2 * FF_TN), F32),
                        pltpu.VMEM((TM, FF_TN), BF16)],
        compiler_params=_params(("arbitrary", "arbitrary")),
        name=f"ffn{layer}",
    )(x, mod, mod, mod, norm_g4, w_up, w_up, conv_w, conv_w, conv_b3, conv_b3, w_down)


def _lru_kernel(x_ref, sh_ref, sc_ref, gt_ref, ng_ref, wig_ref, wir_ref, cw_ref, cb_ref, wgt_ref, bgt_ref,
                lam_ref, h0_ref, wout_ref, o_ref, st_ref,
                h_scr, yg_scr, perm_scr, gg_scr, af_scr, bf_scr, ab_scr, bb_scr):
    i = pl.program_id(0)
    n = pl.program_id(1)

    @pl.when(n == 0)
    def _():
        _modulate_seg_major(x_ref, ng_ref, sh_ref, sc_ref, perm_scr, h_scr)

    h = h_scr[...]
    is_start, is_end = _seg_masks(i, LRU_TN)
    gg_scr[...] = _gelu(_dot(h, wig_ref[...].astype(BF16)))
    rec = _dot(h, wir_ref[...].astype(BF16))
    prev1 = _shift_prev(rec, is_start)
    prev2 = _shift_prev(prev1, is_start)
    xc = (cb_ref[...] + prev2 * cw_ref[0:1, :] + prev1 * cw_ref[1:2, :] + rec * cw_ref[2:3, :]
          + _shift_next(rec, is_end) * cw_ref[3:4, :])

    lam = lam_ref[...]
    softplus = jnp.maximum(-lam, 0.0) + jnp.log1p(jnp.exp(-jnp.abs(lam)))
    for e in range(LRU_TN // LRU_BLOCK):
        cols = slice(e * LRU_BLOCK, (e + 1) * LRU_BLOCK)
        xce = xc[:, cols]
        w4 = jnp.concatenate([wgt_ref[d, g, e] for d in range(2) for g in range(2)], axis=1)
        b4 = jnp.concatenate([bgt_ref[d, g, e] for d in range(2) for g in range(2)], axis=1)
        gates = _sigmoid(_dot(xce.astype(BF16), w4.astype(BF16)) + b4)
        for d, (a_scr, b_scr) in enumerate(((af_scr, bf_scr), (ab_scr, bb_scr))):
            r = gates[:, (2 * d) * LRU_BLOCK:(2 * d + 1) * LRU_BLOCK]
            ig = gates[:, (2 * d + 1) * LRU_BLOCK:(2 * d + 2) * LRU_BLOCK]
            log_a = (-LRU_C * r) * softplus[d:d + 1, cols]
            a = jnp.exp(log_a)
            a_scr[:, cols] = a
            b_scr[:, cols] = jnp.sqrt(jnp.maximum(1.0 - a * a, 0.0)) * (ig * xce)

    def local_scan(t, carry):
        hf, cf, hb, cb = carry
        rf = pl.multiple_of(t * N_SEG, N_SEG)
        rb = pl.multiple_of((SEG - 1 - t) * N_SEG, N_SEG)
        a = af_scr[pl.ds(rf, N_SEG), :]
        hf = a * hf + bf_scr[pl.ds(rf, N_SEG), :]
        cf = a * cf
        bf_scr[pl.ds(rf, N_SEG), :] = hf
        af_scr[pl.ds(rf, N_SEG), :] = cf
        a = ab_scr[pl.ds(rb, N_SEG), :]
        hb = a * hb + bb_scr[pl.ds(rb, N_SEG), :]
        cb = a * cb
        bb_scr[pl.ds(rb, N_SEG), :] = hb
        ab_scr[pl.ds(rb, N_SEG), :] = cb
        return hf, cf, hb, cb

    zeros = jnp.zeros((N_SEG, LRU_TN), F32)
    ones = jnp.ones((N_SEG, LRU_TN), F32)
    end_f, dec_f, end_b, dec_b = lax.fori_loop(0, SEG, local_scan, (zeros, ones, zeros, ones), unroll=8)

    per_seq = jnp.where(i < N_PROMPT_TILES, SEQ // SEG, DEC_SEQ // SEG)
    sub = lax.broadcasted_iota(jnp.int32, (N_SEG, LRU_TN), 0)
    h0f = h0_ref[0:1, :]
    h0b = h0_ref[1:2, :]
    in_f, in_b = zeros, zeros
    out_f, out_b = [None] * N_SEG, [None] * N_SEG
    for r in range(N_SEG):
        hin = h0f if r == 0 else jnp.where((r & (per_seq - 1)) == 0, h0f, out_f[r - 1])
        in_f = jnp.where(sub == r, hin, in_f)
        out_f[r] = end_f[r:r + 1, :] + dec_f[r:r + 1, :] * hin
    for r in reversed(range(N_SEG)):
        hin = h0b if r == N_SEG - 1 else jnp.where((r & (per_seq - 1)) == per_seq - 1, h0b, out_b[r + 1])
        in_b = jnp.where(sub == r, hin, in_b)
        out_b[r] = end_b[r:r + 1, :] + dec_b[r:r + 1, :] * hin
    for q in range(SEQ_PER_TILE):
        st_ref[q, 0:1, :] = out_f[2 * q + 1]
        st_ref[q, 1:2, :] = out_b[2 * q]

    def every_step(v):
        return jnp.broadcast_to(v[None], (SEG, N_SEG, LRU_TN)).reshape(TM, LRU_TN)

    hsum = ((bf_scr[...] + af_scr[...] * every_step(in_f))
            + (bb_scr[...] + ab_scr[...] * every_step(in_b)))
    col = pl.multiple_of(n * LRU_TN, LRU_TN)
    yg_scr[:, pl.ds(col, LRU_TN)] = (gg_scr[...] * hsum).astype(BF16)

    @pl.when(n == D_RNN // LRU_TN - 1)
    def _():
        y = _dot(yg_scr[...], wout_ref[...].astype(BF16))
        for cb in range(D_MODEL // LANES):
            perm_scr[cb] = y[:, cb * LANES:(cb + 1) * LANES]
        _residual_from_seg_major(x_ref, gt_ref, perm_scr, o_ref)


def _lru_mixer(x, mod, layer, j, norm_g4, state_lru, w_in, conv_w, conv_b, w_gate, b_gate, lam, w_out):
    n_steps = D_RNN // LRU_TN
    per_step = LRU_TN // LRU_BLOCK
    n_lru = w_in.shape[0]
    h0 = jnp.concatenate([jnp.zeros((N_PROMPT_TILES, 2, D_RNN), F32), state_lru[:, j]], axis=0)
    out, st = pl.pallas_call(
        _lru_kernel,
        grid=(N_TILES, n_steps),
        in_specs=[
            pl.BlockSpec((TM, D_MODEL), lambda i, n: (i, 0)),
            _mod_spec(layer, 0), _mod_spec(layer, 1), _mod_spec(layer, 2),
            _ng_spec(layer, 0),
            pl.BlockSpec((None, D_MODEL, LRU_TN), lambda i, n: (j, 0, n)),
            pl.BlockSpec((None, D_MODEL, LRU_TN), lambda i, n: (j, 0, n_steps + n)),
            pl.BlockSpec((None, LRU_CONV_W, LRU_TN), lambda i, n: (j, 0, n)),
            pl.BlockSpec((None, 1, LRU_TN), lambda i, n: (j, 0, n)),
            pl.BlockSpec((None, 2, 2, per_step, LRU_BLOCK, LRU_BLOCK), lambda i, n: (j, 0, 0, n, 0, 0)),
            pl.BlockSpec((None, 2, 2, per_step, 1, LRU_BLOCK), lambda i, n: (j, 0, 0, n, 0, 0)),
            pl.BlockSpec((None, 2, LRU_TN), lambda i, n: (j, 0, n)),
            pl.BlockSpec((None, 2, LRU_TN), lambda i, n: (i, 0, n)),
            pl.BlockSpec((None, D_RNN, D_MODEL), lambda i, n: (j, 0, 0), pipeline_mode=pl.Buffered(1)),
        ],
        out_specs=[
            pl.BlockSpec((TM, D_MODEL), lambda i, n: (i, 0)),
            pl.BlockSpec((SEQ_PER_TILE, 2, LRU_TN), lambda i, n: (i, 0, n)),
        ],
        out_shape=[
            jax.ShapeDtypeStruct((N_ROWS, D_MODEL), F32),
            jax.ShapeDtypeStruct((N_TILES * SEQ_PER_TILE, 2, D_RNN), F32),
        ],
        scratch_shapes=[pltpu.VMEM((TM, D_MODEL), BF16), pltpu.VMEM((TM, D_RNN), BF16),
                        pltpu.VMEM((D_MODEL // LANES, TM, LANES), F32)]
        + [pltpu.VMEM((TM, LRU_TN), F32)] * 5,
        compiler_params=_params(("arbitrary", "arbitrary")),
        name=f"lru{layer}",
    )(x, mod, mod, mod, norm_g4, w_in, w_in, conv_w, conv_b.reshape(n_lru, 1, D_RNN), w_gate,
      b_gate.reshape(n_lru, 2, 2, LRU_BLOCKS, 1, LRU_BLOCK), lam, h0, w_out)
    return out, st[:BATCH]


def _cmlp_kernel(x_ref, sh_ref, sc_ref, gt_ref, ng_ref, win_ref, bin_ref, vng_ref, ws_ref, bs_ref,
                 wout_ref, o_ref, u_scr, v_scr, uv_scr):
    gw = D_B // G_B
    h = _modulate(x_ref[...], ng_ref[...], sh_ref[...], sc_ref[...]).astype(BF16)
    ssq = jnp.zeros((TM_B, 1), F32)
    for g in range(G_B):
        cu = slice(g * gw, (g + 1) * gw)
        cv = slice(D_B + g * gw, D_B + (g + 1) * gw)
        u_scr[:, cu] = _gelu(_dot(h, win_ref[:, cu]) + bin_ref[:, cu])
        v = _gelu(_dot(h, win_ref[:, cv]) + bin_ref[:, cv])
        v_scr[:, cu] = v
        ssq = ssq + jnp.sum(v * v, axis=-1, keepdims=True)
    rinv = lax.rsqrt(ssq * (1.0 / D_B) + EPS)
    for g in range(G_B):
        cu = slice(g * gw, (g + 1) * gw)
        vn = ((v_scr[:, cu] * rinv) * vng_ref[:, cu]).astype(BF16)
        w_s = ws_ref[g]
        for c in range(TM_B // CHUNK):
            rows = slice(c * CHUNK, (c + 1) * CHUNK)
            sv = _dot(w_s, vn[rows, :]) + bs_ref[:, cu]
            uv_scr[rows, cu] = (u_scr[rows, cu] * sv).astype(BF16)
    o_ref[...] = x_ref[...] + gt_ref[...] * _dot(uv_scr[...], wout_ref[...])


def _cmlp_mixer(x, mod, layer, ng, w_in, b_in, vnorm_g, w_s, b_s, w_out):
    gw = D_B // G_B
    bs_full = jnp.repeat(b_s.T, gw, axis=1)
    const = lambda *shape: pl.BlockSpec(shape, lambda i: (0,) * len(shape), pipeline_mode=pl.Buffered(1))
    return pl.pallas_call(
        _cmlp_kernel,
        grid=(N_ROWS // TM_B,),
        in_specs=[
            pl.BlockSpec((TM_B, D_MODEL), lambda i: (i, 0)),
            _mod_spec(layer, 0, TM_B), _mod_spec(layer, 1, TM_B), _mod_spec(layer, 2, TM_B),
            const(1, D_MODEL),
            const(D_MODEL, 2 * D_B),
            const(1, 2 * D_B),
            const(1, D_B),
            const(G_B, CHUNK, CHUNK),
            const(CHUNK, D_B),
            const(D_B, D_MODEL),
        ],
        out_specs=pl.BlockSpec((TM_B, D_MODEL), lambda i: (i, 0)),
        out_shape=jax.ShapeDtypeStruct((N_ROWS, D_MODEL), F32),
        scratch_shapes=[pltpu.VMEM((TM_B, D_B), F32), pltpu.VMEM((TM_B, D_B), F32),
                        pltpu.VMEM((TM_B, D_B), BF16)],
        compiler_params=_params(("arbitrary",)),
        name=f"cmlp{layer}",
    )(x, mod, mod, mod, ng.reshape(1, D_MODEL), w_in.astype(BF16), b_in.reshape(1, 2 * D_B),
      vnorm_g.reshape(1, D_B), w_s.astype(BF16), bs_full, w_out.astype(BF16))


def _qkv_kernel(x_ref, sh_ref, sc_ref, ng_ref, w_ref, qkg_ref, gsum_ref, cos_ref, sin_ref,
                q_ref, kp_ref, ks_ref, vp_ref, vs_ref):
    i = pl.program_id(0)
    h = _modulate(x_ref[...], ng_ref[...], sh_ref[...], sc_ref[...]).astype(BF16)
    lane = lax.broadcasted_iota(jnp.int32, (TM_Q, LANES), 1)
    first_half = (lane & (HEAD_DIM_C - 1)) < HEAD_DIM_C // 2

    def norm_store(part, out_ref, rope, out_scale):
        z = _dot(h, w_ref[:, part * D_MODEL:(part + 1) * D_MODEL])
        for hb in range(N_HEADS_C):
            zb = z[:, hb * LANES:(hb + 1) * LANES]
            sq = zb * zb
            hi = sq.astype(BF16)
            lo = (sq - hi.astype(F32)).astype(BF16)
            ssum = _dot(hi, gsum_ref[...]) + _dot(lo, gsum_ref[...])
            zb = zb * lax.rsqrt(ssum * (1.0 / HEAD_DIM_C) + EPS) * qkg_ref[part:part + 1, :]
            if rope:
                partner = jnp.where(first_half, pltpu.roll(zb, LANES - HEAD_DIM_C // 2, 1),
                                    pltpu.roll(zb, HEAD_DIM_C // 2, 1))
                zb = zb * cos_ref[...] + partner * sin_ref[...]
            if out_scale != 1.0:
                zb = zb * out_scale
            out_ref[:, hb * LANES:(hb + 1) * LANES] = zb.astype(out_ref.dtype)

    def qkv(rope, k_ref, v_ref):
        norm_store(0, q_ref, rope, HEAD_DIM_C ** -0.5)
        norm_store(1, k_ref, rope, 1.0)
        v_ref[...] = _dot(h, w_ref[:, 2 * D_MODEL:])

    @pl.when(i < N_PROMPT // TM_Q)
    def _():
        qkv(False, kp_ref, vp_ref)

    @pl.when(i >= N_PROMPT // TM_Q)
    def _():
        qkv(True, ks_ref, vs_ref)


def _attn_heads(q_ref, k_ref, v_ref, lam_ref, sg_ref, o_ref, lam_init):
    lv = lam_ref[...]
    lam = (jnp.exp(jnp.sum(lv[0:1, :] * lv[1:2, :], axis=-1, keepdims=True))
           - jnp.exp(jnp.sum(lv[2:3, :] * lv[3:4, :], axis=-1, keepdims=True)) + lam_init)
    lane = lax.broadcasted_iota(jnp.int32, (TQ, LANES), 1)
    comp0 = lane < HEAD_DIM_C
    for hd in range(N_HEADS_C):
        cols = slice(hd * LANES, (hd + 1) * LANES)
        qh = q_ref[:, cols]
        kh = k_ref[:, cols]
        vh = v_ref[:, cols]
        zero = jnp.zeros_like(qh)
        s0 = _dot_nt(jnp.where(comp0, qh, zero), kh)
        s1 = _dot_nt(jnp.where(comp0, zero, qh), kh)
        e0 = jnp.exp(s0 - jnp.max(s0, axis=-1, keepdims=True))
        e1 = jnp.exp(s1 - jnp.max(s1, axis=-1, keepdims=True))
        inv0 = 1.0 / jnp.sum(e0, axis=-1, keepdims=True)
        inv1 = lam / jnp.sum(e1, axis=-1, keepdims=True)
        w = e0 * inv0 - e1 * inv1
        o = _dot(w.astype(BF16), vh)
        o = o * lax.rsqrt(jnp.mean(o * o, axis=-1, keepdims=True) + EPS) * sg_ref[...]
        o_ref[:, cols] = (o * (1.0 - lam_init)).astype(BF16)


def _attn_prompt_kernel(q_ref, k_ref, v_ref, lam_ref, sg_ref, o_ref, k_scr, v_scr, *, lam_init):
    k_scr[...] = k_ref[...].astype(BF16)
    v_scr[...] = v_ref[...].astype(BF16)
    _attn_heads(q_ref, k_scr, v_scr, lam_ref, sg_ref, o_ref, lam_init)


def _attn_sample_kernel(q_ref, kl_ref, vl_ref, kc_ref, vc_ref, lam_ref, sg_ref, o_ref, k_scr, v_scr,
                        *, lam_init):
    @pl.when(pl.program_id(1) == 0)
    def _():
        k_scr[0:PAST_LEN, :] = kc_ref[...].astype(BF16)
        k_scr[PAST_LEN:, :] = kl_ref[...].astype(BF16)
        v_scr[0:PAST_LEN, :] = vc_ref[...].astype(BF16)
        v_scr[PAST_LEN:, :] = vl_ref[...].astype(BF16)

    _attn_heads(q_ref, k_scr, v_scr, lam_ref, sg_ref, o_ref, lam_init)


def _attn_out_kernel(x_ref, gt_ref, op_ref, os_ref, w_ref, o_ref):
    i = pl.program_id(0)
    w = w_ref[...].astype(BF16)

    @pl.when(i < N_PROMPT_TILES)
    def _():
        o_ref[...] = x_ref[...] + gt_ref[...] * _dot(op_ref[...], w)

    @pl.when(i >= N_PROMPT_TILES)
    def _():
        o_ref[...] = x_ref[...] + gt_ref[...] * _dot(os_ref[...], w)


def _rope_tables():
    t = jnp.arange(DEC_SEQ)
    inv = ROPE_BASE ** (-jnp.arange(N_FREQ_AXIS, dtype=F32) / N_FREQ_AXIS)
    ang = jnp.concatenate([(t // GRID_W)[:, None] * inv, (t % GRID_W)[:, None] * inv], axis=-1)
    cos, sin = jnp.cos(ang), jnp.sin(ang)
    cos64 = jnp.concatenate([cos, cos], axis=-1)
    sin64 = jnp.concatenate([-sin, sin], axis=-1)
    return jnp.tile(cos64, (1, 2)), jnp.tile(sin64, (1, 2))


def _attn_mixer(x, mod, layer, ng, cache_k, cache_v, w_qkv, qk_g, lam_vec, subln_g, w_out):
    lam_init = 0.8 - 0.6 * math.exp(-0.3 * layer)
    cos_t, sin_t = _rope_tables()
    qkg = jnp.tile(qk_g, (1, 2))
    lane = jnp.arange(LANES)
    gsum = (lane[:, None] // HEAD_DIM_C == lane[None, :] // HEAD_DIM_C).astype(BF16)
    n_prompt_q = N_PROMPT // TM_Q
    per_seq = DEC_SEQ // TM_Q
    prm = lambda i: (jnp.minimum(i, n_prompt_q - 1), 0)
    smp = lambda i: (jnp.maximum(i - n_prompt_q, 0), 0)
    rope_blk = lambda i: (i % per_seq, 0)
    q, kp, ks, vp, vs = pl.pallas_call(
        _qkv_kernel,
        grid=(N_ROWS // TM_Q,),
        in_specs=[
            pl.BlockSpec((TM_Q, D_MODEL), lambda i: (i, 0)),
            _mod_spec(layer, 0, TM_Q), _mod_spec(layer, 1, TM_Q),
            pl.BlockSpec((1, D_MODEL), lambda i: (0, 0)),
            pl.BlockSpec((D_MODEL, 3 * D_MODEL), lambda i: (0, 0), pipeline_mode=pl.Buffered(1)),
            pl.BlockSpec((2, LANES), lambda i: (0, 0)),
            pl.BlockSpec((LANES, LANES), lambda i: (0, 0)),
            pl.BlockSpec((TM_Q, LANES), rope_blk),
            pl.BlockSpec((TM_Q, LANES), rope_blk),
        ],
        out_specs=[
            pl.BlockSpec((TM_Q, D_MODEL), lambda i: (i, 0)),
            pl.BlockSpec((TM_Q, D_MODEL), prm), pl.BlockSpec((TM_Q, D_MODEL), smp),
            pl.BlockSpec((TM_Q, D_MODEL), prm), pl.BlockSpec((TM_Q, D_MODEL), smp),
        ],
        out_shape=[
            jax.ShapeDtypeStruct((N_ROWS, D_MODEL), BF16),
            jax.ShapeDtypeStruct((N_PROMPT, D_MODEL), F32), jax.ShapeDtypeStruct((N_SAMPLE, D_MODEL), F32),
            jax.ShapeDtypeStruct((N_PROMPT, D_MODEL), F32), jax.ShapeDtypeStruct((N_SAMPLE, D_MODEL), F32),
        ],
        compiler_params=_params(("arbitrary",)),
        name=f"qkv{layer}",
    )(x, mod, mod, ng.reshape(1, D_MODEL), w_qkv.astype(BF16), qkg, gsum, cos_t, sin_t)

    sg = subln_g.reshape(1, 2 * HEAD_DIM_C)
    o_p = pl.pallas_call(
        functools.partial(_attn_prompt_kernel, lam_init=lam_init),
        grid=(BATCH,),
        in_specs=[
            pl.BlockSpec((SEQ, D_MODEL), lambda b: (b, 0)),
            pl.BlockSpec((SEQ, D_MODEL), lambda b: (b, 0)),
            pl.BlockSpec((SEQ, D_MODEL), lambda b: (b, 0)),
            pl.BlockSpec((4, HEAD_DIM_C), lambda b: (0, 0)),
            pl.BlockSpec((1, 2 * HEAD_DIM_C), lambda b: (0, 0)),
        ],
        out_specs=pl.BlockSpec((SEQ, D_MODEL), lambda b: (b, 0)),
        out_shape=jax.ShapeDtypeStruct((N_PROMPT, D_MODEL), BF16),
        scratch_shapes=[pltpu.VMEM((SEQ, D_MODEL), BF16), pltpu.VMEM((SEQ, D_MODEL), BF16)],
        compiler_params=_params(("arbitrary",)),
        name=f"attn_prompt{layer}",
    )(q, kp, vp, lam_vec, sg)

    n_qb = DEC_SEQ // TQ
    t_all = PAST_LEN + DEC_SEQ
    o_s = pl.pallas_call(
        functools.partial(_attn_sample_kernel, lam_init=lam_init),
        grid=(DEC_BATCH, n_qb),
        in_specs=[
            pl.BlockSpec((TQ, D_MODEL), lambda b, t: (N_PROMPT // TQ + b * n_qb + t, 0)),
            pl.BlockSpec((DEC_SEQ, D_MODEL), lambda b, t: (b, 0)),
            pl.BlockSpec((DEC_SEQ, D_MODEL), lambda b, t: (b, 0)),
            pl.BlockSpec((None, PAST_LEN, D_MODEL), lambda b, t: (b, 0, 0)),
            pl.BlockSpec((None, PAST_LEN, D_MODEL), lambda b, t: (b, 0, 0)),
            pl.BlockSpec((4, HEAD_DIM_C), lambda b, t: (0, 0)),
            pl.BlockSpec((1, 2 * HEAD_DIM_C), lambda b, t: (0, 0)),
        ],
        out_specs=pl.BlockSpec((TQ, D_MODEL), lambda b, t: (b * n_qb + t, 0)),
        out_shape=jax.ShapeDtypeStruct((N_SAMPLE, D_MODEL), BF16),
        scratch_shapes=[pltpu.VMEM((t_all, D_MODEL), BF16), pltpu.VMEM((t_all, D_MODEL), BF16)],
        compiler_params=_params(("arbitrary", "arbitrary")),
        name=f"attn_sample{layer}",
    )(q, ks, vs, cache_k.reshape(DEC_BATCH, PAST_LEN, D_MODEL), cache_v.reshape(DEC_BATCH, PAST_LEN, D_MODEL),
      lam_vec, sg)

    x_new = pl.pallas_call(
        _attn_out_kernel,
        grid=(N_TILES,),
        in_specs=[
            pl.BlockSpec((TM, D_MODEL), lambda i: (i, 0)),
            _mod_spec(layer, 2),
            pl.BlockSpec((TM, D_MODEL), lambda i: (jnp.minimum(i, N_PROMPT_TILES - 1), 0)),
            pl.BlockSpec((TM, D_MODEL), lambda i: (jnp.maximum(i - N_PROMPT_TILES, 0), 0)),
            pl.BlockSpec((D_MODEL, D_MODEL), lambda i: (0, 0)),
        ],
        out_specs=pl.BlockSpec((TM, D_MODEL), lambda i: (i, 0)),
        out_shape=jax.ShapeDtypeStruct((N_ROWS, D_MODEL), F32),
        compiler_params=_params(("arbitrary",)),
        name=f"attn_out{layer}",
    )(x, mod, o_p, o_s, w_out)
    return x_new, kp, vp


def kernel(x_prompt, x_sample, state_lru, cache_k, cache_v, c, c_ctx, w_mod, b_mod, norm_g, lru_w_in, lru_conv_w, lru_conv_b, lru_w_gate, lru_b_gate, lru_lambda, lru_w_out, cmlp_w_in, cmlp_b_in, cmlp_norm_g, cmlp_w_s, cmlp_b_s, cmlp_w_out, attn_w_qkv, attn_qk_g, attn_lambda, attn_subln_g, attn_w_out, ffn_w_up, ffn_conv_w, ffn_conv_b, ffn_w_down):
    x = jnp.concatenate([x_prompt.reshape(N_PROMPT, D_MODEL), x_sample.reshape(N_SAMPLE, D_MODEL)], axis=0)
    cond8 = jnp.concatenate([c_ctx[None, :], c, jnp.zeros((SUBLANES - 1 - DEC_BATCH, D_MODEL), F32)], axis=0)
    mod = _adaln(cond8, w_mod, b_mod)
    norm_g4 = norm_g.reshape(DEPTH, 2, 1, D_MODEL)
    new_lru, new_k, new_v = [], [], []
    for l in range(DEPTH):
        kind, j = l % 3, l // 3
        if kind == 0:
            x, st = _lru_mixer(x, mod, l, j, norm_g4, state_lru, lru_w_in, lru_conv_w, lru_conv_b,
                               lru_w_gate, lru_b_gate, lru_lambda, lru_w_out)
            new_lru.append(st)
        elif kind == 1:
            x = _cmlp_mixer(x, mod, l, norm_g[l, 0], cmlp_w_in[j], cmlp_b_in[j], cmlp_norm_g[j],
                            cmlp_w_s[j], cmlp_b_s[j], cmlp_w_out[j])
        else:
            x, kp, vp = _attn_mixer(x, mod, l, norm_g[l, 0], cache_k[:, j], cache_v[:, j], attn_w_qkv[j],
                                    attn_qk_g[j], attn_lambda[j], attn_subln_g[j], attn_w_out[j])
            new_k.append(kp.reshape(BATCH, SEQ, N_HEADS_C, 2, HEAD_DIM_C))
            new_v.append(vp.reshape(BATCH, SEQ, N_HEADS_C, 2 * HEAD_DIM_C))
        x = _ffn(x, mod, l, norm_g4, ffn_w_up, ffn_conv_w, ffn_conv_b, ffn_w_down)
    y_prompt = x[:N_PROMPT].reshape(BATCH, SEQ, D_MODEL)
    y_sample = x[N_PROMPT:].reshape(DEC_BATCH, DEC_SEQ, D_MODEL)
    return (y_prompt, y_sample, jnp.stack(new_lru, axis=1), jnp.stack(new_k, axis=1), jnp.stack(new_v, axis=1))
```

```python
import functools
import math

import jax
import jax.numpy as jnp
from jax import lax
from jax.experimental import pallas as pl
from jax.experimental.pallas import tpu as pltpu

D_MODEL = 1024
BATCH = 32
SEQ = 256
DEPTH = 4
DEC_BATCH = 2
DEC_SEQ = 1024
PAST_LEN = 512
GRID_W = 64
N_MOD = 6
EPS = 1e-6
D_RNN = 1280
LRU_BLOCKS = 10
LRU_BLOCK = 128
LRU_CONV_W = 4
LRU_CONV_LEFT = 2
LRU_C = 8.0
D_B = 2 * D_MODEL
CHUNK = 128
G_B = 8
N_HEADS_C = 8
HEAD_DIM_C = 64
N_FREQ_AXIS = 16
ROPE_BASE = 10000.0
D_FF = 2816
FFN_CONV_W = 3

N_PROMPT = BATCH * SEQ
N_SAMPLE = DEC_BATCH * DEC_SEQ
N_ROWS = N_PROMPT + N_SAMPLE
TM = 1024
N_TILES = N_ROWS // TM
N_PROMPT_TILES = N_PROMPT // TM
SEQ_PER_TILE = TM // SEQ
SUBLANES = 8
LANES = 128
N_SEG = SUBLANES
SEG = TM // N_SEG
FF_TN = 256
FF_BLK = 128
LRU_TN = 256
N_FF_CHUNKS = D_FF // FF_TN
TM_B = 512
TM_Q = 512
TQ = 256
VMEM_LIMIT = 56 * 1024 * 1024

BF16 = jnp.bfloat16
F32 = jnp.float32


def _dot(a, b):
    return jnp.dot(a, b, preferred_element_type=F32)


def _dot_nt(a, b):
    return lax.dot_general(a, b, (((1,), (1,)), ((), ())), preferred_element_type=F32)


def _mod_row(i):
    return jnp.maximum(i - (N_PROMPT_TILES - 1), 0)


def _modulate(x, ng, sh, sc):
    ms = jnp.mean(x * x, axis=-1, keepdims=True)
    return (x * lax.rsqrt(ms + EPS) * ng) * (1.0 + sc) + sh


def _gelu(x):
    k = math.sqrt(2.0 / math.pi)
    return (0.5 * x) * (1.0 + jnp.tanh(x * (k + (k * 0.044715) * (x * x))))


def _sigmoid(x):
    return 0.5 * jnp.tanh(0.5 * x) + 0.5


def _params(sem):
    return pltpu.CompilerParams(dimension_semantics=sem, vmem_limit_bytes=VMEM_LIMIT)


def _adaln_kernel(cond_ref, w_ref, b_ref, o_ref):
    cnd = cond_ref[...]
    s = (cnd * _sigmoid(cnd)).astype(BF16)
    o_ref[...] = _dot(s, w_ref[...].astype(BF16)) + b_ref[...]


def _adaln(cond8, w_mod, b_mod):
    out = pl.pallas_call(
        _adaln_kernel,
        grid=(DEPTH, N_MOD),
        in_specs=[
            pl.BlockSpec((SUBLANES, D_MODEL), lambda l, k: (0, 0)),
            pl.BlockSpec((None, D_MODEL, D_MODEL), lambda l, k: (l, 0, k)),
            pl.BlockSpec((None, None, 1, D_MODEL), lambda l, k: (l, k, 0, 0)),
        ],
        out_specs=pl.BlockSpec((None, None, SUBLANES, D_MODEL), lambda l, k: (l, k, 0, 0)),
        out_shape=jax.ShapeDtypeStruct((DEPTH, N_MOD, SUBLANES, D_MODEL), F32),
        compiler_params=_params(("arbitrary", "arbitrary")),
        name="adaln",
    )(cond8, w_mod, b_mod.reshape(DEPTH, N_MOD, 1, D_MODEL))
    return out.reshape(DEPTH, N_MOD, SUBLANES, 1, D_MODEL)


def _mod_spec(layer, k, tile_rows=TM):
    per = TM // tile_rows
    return pl.BlockSpec((None, None, None, 1, D_MODEL),
                        lambda i, *_: (layer, k, _mod_row(i // per), 0, 0))


def _ng_spec(layer, which):
    return pl.BlockSpec((None, None, 1, D_MODEL), lambda i, *_: (layer, which, 0, 0))


def _modulate_seg_major(x_ref, ng_ref, sh_ref, sc_ref, perm_scr, h_scr):
    for r in range(N_SEG):
        rows = slice(r * SEG, (r + 1) * SEG)
        h = _modulate(x_ref[rows, :], ng_ref[...], sh_ref[...], sc_ref[...])
        for cb in range(D_MODEL // LANES):
            perm_scr[cb, pl.ds(r, SEG, stride=N_SEG), :] = h[:, cb * LANES:(cb + 1) * LANES]
    for cb in range(D_MODEL // LANES):
        h_scr[:, cb * LANES:(cb + 1) * LANES] = perm_scr[cb].astype(BF16)


def _residual_from_seg_major(x_ref, gt_ref, perm_scr, o_ref):
    for r in range(N_SEG):
        rows = slice(r * SEG, (r + 1) * SEG)
        for cb in range(D_MODEL // LANES):
            cols = slice(cb * LANES, (cb + 1) * LANES)
            y = perm_scr[cb, pl.ds(r, SEG, stride=N_SEG), :]
            o_ref[rows, cols] = x_ref[rows, cols] + gt_ref[:, cols] * y


def _seg_masks(i, width):
    per_seq = jnp.where(i < N_PROMPT_TILES, SEQ // SEG, DEC_SEQ // SEG)
    r = lax.broadcasted_iota(jnp.int32, (N_SEG, width), 0) & (per_seq - 1)
    return r == 0, r == per_seq - 1


def _shift_prev(z, is_start):
    wrap = jnp.where(is_start, 0.0, pltpu.roll(z[TM - N_SEG:, :], 1, 0))
    return jnp.concatenate([wrap, z[:TM - N_SEG, :]], axis=0)


def _shift_next(z, is_end):
    wrap = jnp.where(is_end, 0.0, pltpu.roll(z[:N_SEG, :], N_SEG - 1, 0))
    return jnp.concatenate([z[N_SEG:, :], wrap], axis=0)


def _ffn_kernel(x_ref, sh_ref, sc_ref, gt_ref, ng_ref, wg_ref, wu_ref, cwg_ref, cwu_ref,
                cbg_ref, cbu_ref, wd_ref, o_ref, h_scr, acc_scr, w_scr, z_scr, act_scr):
    i = pl.program_id(0)
    j = pl.program_id(1)

    @pl.when(j == 0)
    def _():
        _modulate_seg_major(x_ref, ng_ref, sh_ref, sc_ref, acc_scr, h_scr)
        acc_scr[...] = jnp.zeros_like(acc_scr)

    is_start, is_end = _seg_masks(i, 2 * FF_TN)
    w_scr[:, :FF_TN] = wg_ref[...].astype(BF16)
    w_scr[:, FF_TN:] = wu_ref[...].astype(BF16)
    cw = jnp.concatenate([cwg_ref[...], cwu_ref[...]], axis=1)
    cb = jnp.concatenate([cbg_ref[...], cbu_ref[...]], axis=1)

    def up(m):
        r0 = m * FF_BLK
        z_scr[N_SEG + r0:N_SEG + r0 + FF_BLK, :] = _dot(h_scr[r0:r0 + FF_BLK, :], w_scr[...])

    def post(m):
        r0 = N_SEG + m * FF_BLK
        c = (cb + z_scr[r0 - N_SEG:r0 - N_SEG + FF_BLK, :] * cw[0:1, :] + z_scr[r0:r0 + FF_BLK, :] * cw[1:2, :]
             + z_scr[r0 + N_SEG:r0 + N_SEG + FF_BLK, :] * cw[2:3, :])
        hg = 0.5 * c[:, :FF_TN]
        act = (hg * (jnp.tanh(hg) + 1.0)) * c[:, FF_TN:]
        act_scr[m * FF_BLK:(m + 1) * FF_BLK, :] = act.astype(BF16)

    n_blk = TM // FF_BLK
    up(n_blk - 1)
    z_scr[0:N_SEG, :] = jnp.where(is_start, 0.0, pltpu.roll(z_scr[TM:TM + N_SEG, :], 1, 0))
    up(0)
    z_scr[TM + N_SEG:, :] = jnp.where(is_end, 0.0, pltpu.roll(z_scr[N_SEG:2 * N_SEG, :], N_SEG - 1, 0))
    up(1)
    for m in range(n_blk):
        if m + 2 < n_blk - 1:
            up(m + 2)
        post(m)
    y = _dot(act_scr[...], wd_ref[...].astype(BF16))
    for cb_i in range(D_MODEL // LANES):
        acc_scr[cb_i] += y[:, cb_i * LANES:(cb_i + 1) * LANES]

    @pl.when(j == N_FF_CHUNKS - 1)
    def _():
        _residual_from_seg_major(x_ref, gt_ref, acc_scr, o_ref)


N_FFN_INPUTS = 12


def _ffn_kernel_split_out(*refs):
    ins = refs[:N_FFN_INPUTS]
    op_ref, os_ref = refs[N_FFN_INPUTS:N_FFN_INPUTS + 2]
    *scratch, o_scr = refs[N_FFN_INPUTS + 2:]
    _ffn_kernel(*ins, o_scr, *scratch)
    last = pl.program_id(1) == N_FF_CHUNKS - 1
    is_prompt = pl.program_id(0) < N_PROMPT_TILES

    @pl.when(last & is_prompt)
    def _():
        op_ref[...] = o_scr[...]

    @pl.when(last & jnp.logical_not(is_prompt))
    def _():
        os_ref[...] = o_scr[...]


def _ffn(x, mod, layer, norm_g4, w_up, conv_w, conv_b, w_down, split_out=False):
    conv_b3 = conv_b.reshape(DEPTH, 1, 2 * D_FF)
    scratch = [pltpu.VMEM((TM, D_MODEL), BF16), pltpu.VMEM((D_MODEL // LANES, TM, LANES), F32),
               pltpu.VMEM((D_MODEL, 2 * FF_TN), BF16), pltpu.VMEM((TM + 2 * N_SEG, 2 * FF_TN), F32),
               pltpu.VMEM((TM, FF_TN), BF16)]
    if split_out:
        body = _ffn_kernel_split_out
        out_specs = [pl.BlockSpec((TM, D_MODEL), lambda i, j: (jnp.minimum(i, N_PROMPT_TILES - 1), 0)),
                     pl.BlockSpec((TM, D_MODEL), lambda i, j: (jnp.maximum(i - N_PROMPT_TILES, 0), 0))]
        out_shape = [jax.ShapeDtypeStruct((N_PROMPT, D_MODEL), F32), jax.ShapeDtypeStruct((N_SAMPLE, D_MODEL), F32)]
        scratch = scratch + [pltpu.VMEM((TM, D_MODEL), F32)]
    else:
        body = _ffn_kernel
        out_specs = pl.BlockSpec((TM, D_MODEL), lambda i, j: (i, 0))
        out_shape = jax.ShapeDtypeStruct((N_ROWS, D_MODEL), F32)
    return pl.pallas_call(
        body,
        grid=(N_TILES, N_FF_CHUNKS),
        in_specs=[
            pl.BlockSpec((TM, D_MODEL), lambda i, j: (i, 0)),
            _mod_spec(layer, 3), _mod_spec(layer, 4), _mod_spec(layer, 5),
            _ng_spec(layer, 1),
            pl.BlockSpec((None, D_MODEL, FF_TN), lambda i, j: (layer, 0, j)),
            pl.BlockSpec((None, D_MODEL, FF_TN), lambda i, j: (layer, 0, N_FF_CHUNKS + j)),
            pl.BlockSpec((None, FFN_CONV_W, FF_TN), lambda i, j: (layer, 0, j)),
            pl.BlockSpec((None, FFN_CONV_W, FF_TN), lambda i, j: (layer, 0, N_FF_CHUNKS + j)),
            pl.BlockSpec((None, 1, FF_TN), lambda i, j: (layer, 0, j)),
            pl.BlockSpec((None, 1, FF_TN), lambda i, j: (layer, 0, N_FF_CHUNKS + j)),
            pl.BlockSpec((None, FF_TN, D_MODEL), lambda i, j: (layer, j, 0)),
        ],
        out_specs=out_specs,
        out_shape=out_shape,
        scratch_shapes=scratch,
        compiler_params=_params(("arbitrary", "arbitrary")),
        name=f"ffn{layer}",
    )(x, mod, mod, mod, norm_g4, w_up, w_up, conv_w, conv_w, conv_b3, conv_b3, w_down)


def _lru_kernel(x_ref, sh_ref, sc_ref, gt_ref, ng_ref, wig_ref, wir_ref, cw_ref, cb_ref, wgt_ref, bgt_ref,
                lam_ref, h0_ref, wout_ref, o_ref, st_ref,
                h_scr, yg_scr, perm_scr, gg_scr, af_scr, bf_scr, ab_scr, bb_scr):
    i = pl.program_id(0)
    n = pl.program_id(1)

    @pl.when(n == 0)
    def _():
        _modulate_seg_major(x_ref, ng_ref, sh_ref, sc_ref, perm_scr, h_scr)

    h = h_scr[...]
    is_start, is_end = _seg_masks(i, LRU_TN)
    gg_scr[...] = _gelu(_dot(h, wig_ref[...].astype(BF16)))
    rec = _dot(h, wir_ref[...].astype(BF16))
    prev1 = _shift_prev(rec, is_start)
    prev2 = _shift_prev(prev1, is_start)
    xc = (cb_ref[...] + prev2 * cw_ref[0:1, :] + prev1 * cw_ref[1:2, :] + rec * cw_ref[2:3, :]
          + _shift_next(rec, is_end) * cw_ref[3:4, :])

    lam = lam_ref[...]
    softplus = jnp.maximum(-lam, 0.0) + jnp.log1p(jnp.exp(-jnp.abs(lam)))
    for e in range(LRU_TN // LRU_BLOCK):
        cols = slice(e * LRU_BLOCK, (e + 1) * LRU_BLOCK)
        xce = xc[:, cols]
        w4 = jnp.concatenate([wgt_ref[d, g, e] for d in range(2) for g in range(2)], axis=1)
        b4 = jnp.concatenate([bgt_ref[d, g, e] for d in range(2) for g in range(2)], axis=1)
        gates = _sigmoid(_dot(xce.astype(BF16), w4.astype(BF16)) + b4)
        for d, (a_scr, b_scr) in enumerate(((af_scr, bf_scr), (ab_scr, bb_scr))):
            r = gates[:, (2 * d) * LRU_BLOCK:(2 * d + 1) * LRU_BLOCK]
            ig = gates[:, (2 * d + 1) * LRU_BLOCK:(2 * d + 2) * LRU_BLOCK]
            log_a = (-LRU_C * r) * softplus[d:d + 1, cols]
            a = jnp.exp(log_a)
            a_scr[:, cols] = a
            b_scr[:, cols] = jnp.sqrt(jnp.maximum(1.0 - a * a, 0.0)) * (ig * xce)

    def local_scan(t, carry):
        hf, cf, hb, cb = carry
        rf = pl.multiple_of(t * N_SEG, N_SEG)
        rb = pl.multiple_of((SEG - 1 - t) * N_SEG, N_SEG)
        a = af_scr[pl.ds(rf, N_SEG), :]
        hf = a * hf + bf_scr[pl.ds(rf, N_SEG), :]
        cf = a * cf
        bf_scr[pl.ds(rf, N_SEG), :] = hf
        af_scr[pl.ds(rf, N_SEG), :] = cf
        a = ab_scr[pl.ds(rb, N_SEG), :]
        hb = a * hb + bb_scr[pl.ds(rb, N_SEG), :]
        cb = a * cb
        bb_scr[pl.ds(rb, N_SEG), :] = hb
        ab_scr[pl.ds(rb, N_SEG), :] = cb
        return hf, cf, hb, cb

    zeros = jnp.zeros((N_SEG, LRU_TN), F32)
    ones = jnp.ones((N_SEG, LRU_TN), F32)
    end_f, dec_f, end_b, dec_b = lax.fori_loop(0, SEG, local_scan, (zeros, ones, zeros, ones), unroll=8)

    per_seq = jnp.where(i < N_PROMPT_TILES, SEQ // SEG, DEC_SEQ // SEG)
    sub = lax.broadcasted_iota(jnp.int32, (N_SEG, LRU_TN), 0)
    h0f = h0_ref[0:1, :]
    h0b = h0_ref[1:2, :]
    in_f, in_b = zeros, zeros
    out_f, out_b = [None] * N_SEG, [None] * N_SEG
    for r in range(N_SEG):
        hin = h0f if r == 0 else jnp.where((r & (per_seq - 1)) == 0, h0f, out_f[r - 1])
        in_f = jnp.where(sub == r, hin, in_f)
        out_f[r] = end_f[r:r + 1, :] + dec_f[r:r + 1, :] * hin
    for r in reversed(range(N_SEG)):
        hin = h0b if r == N_SEG - 1 else jnp.where((r & (per_seq - 1)) == per_seq - 1, h0b, out_b[r + 1])
        in_b = jnp.where(sub == r, hin, in_b)
        out_b[r] = end_b[r:r + 1, :] + dec_b[r:r + 1, :] * hin
    for q in range(SEQ_PER_TILE):
        st_ref[q, 0:1, :] = out_f[2 * q + 1]
        st_ref[q, 1:2, :] = out_b[2 * q]

    def every_step(v):
        return jnp.broadcast_to(v[None], (SEG, N_SEG, LRU_TN)).reshape(TM, LRU_TN)

    hsum = ((bf_scr[...] + af_scr[...] * every_step(in_f))
            + (bb_scr[...] + ab_scr[...] * every_step(in_b)))
    col = pl.multiple_of(n * LRU_TN, LRU_TN)
    yg_scr[:, pl.ds(col, LRU_TN)] = (gg_scr[...] * hsum).astype(BF16)

    @pl.when(n == D_RNN // LRU_TN - 1)
    def _():
        y = _dot(yg_scr[...], wout_ref[...].astype(BF16))
        for cb in range(D_MODEL // LANES):
            perm_scr[cb] = y[:, cb * LANES:(cb + 1) * LANES]
        _residual_from_seg_major(x_ref, gt_ref, perm_scr, o_ref)


def _lru_kernel_split_in(xp_ref, xs_ref, *refs):
    *rest, x_scr = refs
    first = pl.program_id(1) == 0
    is_prompt = pl.program_id(0) < N_PROMPT_TILES

    @pl.when(first & is_prompt)
    def _():
        x_scr[...] = xp_ref[...]

    @pl.when(first & jnp.logical_not(is_prompt))
    def _():
        x_scr[...] = xs_ref[...]

    _lru_kernel(x_scr, *rest)


def _lru_mixer(x, mod, layer, j, norm_g4, state_lru, w_in, conv_w, conv_b, w_gate, b_gate, lam, w_out):
    n_steps = D_RNN // LRU_TN
    per_step = LRU_TN // LRU_BLOCK
    n_lru = w_in.shape[0]
    h0 = jnp.concatenate([jnp.zeros((N_PROMPT_TILES, 2, D_RNN), F32), state_lru[:, j]], axis=0)
    scratch = ([pltpu.VMEM((TM, D_MODEL), BF16), pltpu.VMEM((TM, D_RNN), BF16),
                pltpu.VMEM((D_MODEL // LANES, TM, LANES), F32)] + [pltpu.VMEM((TM, LRU_TN), F32)] * 5)
    if isinstance(x, tuple):
        body = _lru_kernel_split_in
        x_args = x
        x_specs = [pl.BlockSpec((TM, D_MODEL), lambda i, n: (jnp.minimum(i, N_PROMPT_TILES - 1), 0)),
                   pl.BlockSpec((TM, D_MODEL), lambda i, n: (jnp.maximum(i - N_PROMPT_TILES, 0), 0))]
        scratch = scratch + [pltpu.VMEM((TM, D_MODEL), F32)]
    else:
        body = _lru_kernel
        x_args = (x,)
        x_specs = [pl.BlockSpec((TM, D_MODEL), lambda i, n: (i, 0))]
    out, st = pl.pallas_call(
        body,
        grid=(N_TILES, n_steps),
        in_specs=x_specs + [
            _mod_spec(layer, 0), _mod_spec(layer, 1), _mod_spec(layer, 2),
            _ng_spec(layer, 0),
            pl.BlockSpec((None, D_MODEL, LRU_TN), lambda i, n: (j, 0, n)),
            pl.BlockSpec((None, D_MODEL, LRU_TN), lambda i, n: (j, 0, n_steps + n)),
            pl.BlockSpec((None, LRU_CONV_W, LRU_TN), lambda i, n: (j, 0, n)),
            pl.BlockSpec((None, 1, LRU_TN), lambda i, n: (j, 0, n)),
            pl.BlockSpec((None, 2, 2, per_step, LRU_BLOCK, LRU_BLOCK), lambda i, n: (j, 0, 0, n, 0, 0)),
            pl.BlockSpec((None, 2, 2, per_step, 1, LRU_BLOCK), lambda i, n: (j, 0, 0, n, 0, 0)),
            pl.BlockSpec((None, 2, LRU_TN), lambda i, n: (j, 0, n)),
            pl.BlockSpec((None, 2, LRU_TN), lambda i, n: (i, 0, n)),
            pl.BlockSpec((None, D_RNN, D_MODEL), lambda i, n: (j, 0, 0), pipeline_mode=pl.Buffered(1)),
        ],
        out_specs=[
            pl.BlockSpec((TM, D_MODEL), lambda i, n: (i, 0)),
            pl.BlockSpec((SEQ_PER_TILE, 2, LRU_TN), lambda i, n: (i, 0, n)),
        ],
        out_shape=[
            jax.ShapeDtypeStruct((N_ROWS, D_MODEL), F32),
            jax.ShapeDtypeStruct((N_TILES * SEQ_PER_TILE, 2, D_RNN), F32),
        ],
        scratch_shapes=scratch,
        compiler_params=_params(("arbitrary", "arbitrary")),
        name=f"lru{layer}",
    )(*x_args, mod, mod, mod, norm_g4, w_in, w_in, conv_w, conv_b.reshape(n_lru, 1, D_RNN), w_gate,
      b_gate.reshape(n_lru, 2, 2, LRU_BLOCKS, 1, LRU_BLOCK), lam, h0, w_out)
    return out, st[:BATCH]


def _cmlp_kernel(x_ref, sh_ref, sc_ref, gt_ref, ng_ref, win_ref, bin_ref, vng_ref, ws_ref, bs_ref,
                 wout_ref, o_ref, u_scr, v_scr, uv_scr):
    gw = D_B // G_B
    h = _modulate(x_ref[...], ng_ref[...], sh_ref[...], sc_ref[...]).astype(BF16)
    ssq = jnp.zeros((TM_B, 1), F32)
    for g in range(G_B):
        cu = slice(g * gw, (g + 1) * gw)
        cv = slice(D_B + g * gw, D_B + (g + 1) * gw)
        u_scr[:, cu] = _gelu(_dot(h, win_ref[:, cu]) + bin_ref[:, cu])
        v = _gelu(_dot(h, win_ref[:, cv]) + bin_ref[:, cv])
        v_scr[:, cu] = v
        ssq = ssq + jnp.sum(v * v, axis=-1, keepdims=True)
    rinv = lax.rsqrt(ssq * (1.0 / D_B) + EPS)
    for g in range(G_B):
        cu = slice(g * gw, (g + 1) * gw)
        vn = ((v_scr[:, cu] * rinv) * vng_ref[:, cu]).astype(BF16)
        w_s = ws_ref[g]
        for c in range(TM_B // CHUNK):
            rows = slice(c * CHUNK, (c + 1) * CHUNK)
            sv = _dot(w_s, vn[rows, :]) + bs_ref[:, cu]
            uv_scr[rows, cu] = (u_scr[rows, cu] * sv).astype(BF16)
    o_ref[...] = x_ref[...] + gt_ref[...] * _dot(uv_scr[...], wout_ref[...])


def _cmlp_mixer(x, mod, layer, ng, w_in, b_in, vnorm_g, w_s, b_s, w_out):
    gw = D_B // G_B
    bs_full = jnp.repeat(b_s.T, gw, axis=1)
    const = lambda *shape: pl.BlockSpec(shape, lambda i: (0,) * len(shape), pipeline_mode=pl.Buffered(1))
    return pl.pallas_call(
        _cmlp_kernel,
        grid=(N_ROWS // TM_B,),
        in_specs=[
            pl.BlockSpec((TM_B, D_MODEL), lambda i: (i, 0)),
            _mod_spec(layer, 0, TM_B), _mod_spec(layer, 1, TM_B), _mod_spec(layer, 2, TM_B),
            const(1, D_MODEL),
            const(D_MODEL, 2 * D_B),
            const(1, 2 * D_B),
            const(1, D_B),
            const(G_B, CHUNK, CHUNK),
            const(CHUNK, D_B),
            const(D_B, D_MODEL),
        ],
        out_specs=pl.BlockSpec((TM_B, D_MODEL), lambda i: (i, 0)),
        out_shape=jax.ShapeDtypeStruct((N_ROWS, D_MODEL), F32),
        scratch_shapes=[pltpu.VMEM((TM_B, D_B), F32), pltpu.VMEM((TM_B, D_B), F32),
                        pltpu.VMEM((TM_B, D_B), BF16)],
        compiler_params=_params(("arbitrary",)),
        name=f"cmlp{layer}",
    )(x, mod, mod, mod, ng.reshape(1, D_MODEL), w_in.astype(BF16), b_in.reshape(1, 2 * D_B),
      vnorm_g.reshape(1, D_B), w_s.astype(BF16), bs_full, w_out.astype(BF16))


def _qkv_kernel(x_ref, sh_ref, sc_ref, ng_ref, w_ref, qkg_ref, gsum_ref, cos_ref, sin_ref,
                q_ref, kp_ref, ks_ref, vp_ref, vs_ref):
    i = pl.program_id(0)
    h = _modulate(x_ref[...], ng_ref[...], sh_ref[...], sc_ref[...]).astype(BF16)
    lane = lax.broadcasted_iota(jnp.int32, (TM_Q, LANES), 1)
    first_half = (lane & (HEAD_DIM_C - 1)) < HEAD_DIM_C // 2

    def norm_store(part, out_ref, rope, out_scale):
        z = _dot(h, w_ref[:, part * D_MODEL:(part + 1) * D_MODEL])
        for hb in range(N_HEADS_C):
            zb = z[:, hb * LANES:(hb + 1) * LANES]
            sq = zb * zb
            hi = sq.astype(BF16)
            lo = (sq - hi.astype(F32)).astype(BF16)
            ssum = _dot(hi, gsum_ref[...]) + _dot(lo, gsum_ref[...])
            zb = zb * lax.rsqrt(ssum * (1.0 / HEAD_DIM_C) + EPS) * qkg_ref[part:part + 1, :]
            if rope:
                partner = jnp.where(first_half, pltpu.roll(zb, LANES - HEAD_DIM_C // 2, 1),
                                    pltpu.roll(zb, HEAD_DIM_C // 2, 1))
                zb = zb * cos_ref[...] + partner * sin_ref[...]
            if out_scale != 1.0:
                zb = zb * out_scale
            out_ref[:, hb * LANES:(hb + 1) * LANES] = zb.astype(out_ref.dtype)

    def qkv(rope, k_ref, v_ref):
        norm_store(0, q_ref, rope, HEAD_DIM_C ** -0.5)
        norm_store(1, k_ref, rope, 1.0)
        v_ref[...] = _dot(h, w_ref[:, 2 * D_MODEL:])

    @pl.when(i < N_PROMPT // TM_Q)
    def _():
        qkv(False, kp_ref, vp_ref)

    @pl.when(i >= N_PROMPT // TM_Q)
    def _():
        qkv(True, ks_ref, vs_ref)


def _attn_heads(q_ref, k_ref, v_ref, lam_ref, sg_ref, o_ref, lam_init):
    lv = lam_ref[...]
    lam = (jnp.exp(jnp.sum(lv[0:1, :] * lv[1:2, :], axis=-1, keepdims=True))
           - jnp.exp(jnp.sum(lv[2:3, :] * lv[3:4, :], axis=-1, keepdims=True)) + lam_init)
    lane = lax.broadcasted_iota(jnp.int32, (TQ, LANES), 1)
    comp0 = lane < HEAD_DIM_C
    for hd in range(N_HEADS_C):
        cols = slice(hd * LANES, (hd + 1) * LANES)
        qh = q_ref[:, cols]
        kh = k_ref[:, cols]
        vh = v_ref[:, cols]
        zero = jnp.zeros_like(qh)
        s0 = _dot_nt(jnp.where(comp0, qh, zero), kh)
        s1 = _dot_nt(jnp.where(comp0, zero, qh), kh)
        e0 = jnp.exp(s0 - jnp.max(s0, axis=-1, keepdims=True))
        e1 = jnp.exp(s1 - jnp.max(s1, axis=-1, keepdims=True))
        p0 = e0 / jnp.sum(e0, axis=-1, keepdims=True)
        p1 = e1 / jnp.sum(e1, axis=-1, keepdims=True)
        w = p0 - lam * p1
        o = _dot(w.astype(BF16), vh)
        o = o * lax.rsqrt(jnp.mean(o * o, axis=-1, keepdims=True) + EPS) * sg_ref[...]
        o_ref[:, cols] = (o * (1.0 - lam_init)).astype(BF16)


def _attn_prompt_kernel(q_ref, k_ref, v_ref, lam_ref, sg_ref, o_ref, k_scr, v_scr, *, lam_init):
    k_scr[...] = k_ref[...].astype(BF16)
    v_scr[...] = v_ref[...].astype(BF16)
    _attn_heads(q_ref, k_scr, v_scr, lam_ref, sg_ref, o_ref, lam_init)


def _attn_sample_kernel(q_ref, kl_ref, vl_ref, kc_ref, vc_ref, lam_ref, sg_ref, o_ref, k_scr, v_scr,
                        *, lam_init):
    @pl.when(pl.program_id(1) == 0)
    def _():
        k_scr[0:PAST_LEN, :] = kc_ref[...].astype(BF16)
        k_scr[PAST_LEN:, :] = kl_ref[...].astype(BF16)
        v_scr[0:PAST_LEN, :] = vc_ref[...].astype(BF16)
        v_scr[PAST_LEN:, :] = vl_ref[...].astype(BF16)

    _attn_heads(q_ref, k_scr, v_scr, lam_ref, sg_ref, o_ref, lam_init)


def _attn_out_kernel(x_ref, gt_ref, op_ref, os_ref, w_ref, o_ref):
    i = pl.program_id(0)
    w = w_ref[...].astype(BF16)

    @pl.when(i < N_PROMPT_TILES)
    def _():
        o_ref[...] = x_ref[...] + gt_ref[...] * _dot(op_ref[...], w)

    @pl.when(i >= N_PROMPT_TILES)
    def _():
        o_ref[...] = x_ref[...] + gt_ref[...] * _dot(os_ref[...], w)


def _rope_tables():
    t = jnp.arange(DEC_SEQ)
    inv = ROPE_BASE ** (-jnp.arange(N_FREQ_AXIS, dtype=F32) / N_FREQ_AXIS)
    ang = jnp.concatenate([(t // GRID_W)[:, None] * inv, (t % GRID_W)[:, None] * inv], axis=-1)
    cos, sin = jnp.cos(ang), jnp.sin(ang)
    cos64 = jnp.concatenate([cos, cos], axis=-1)
    sin64 = jnp.concatenate([-sin, sin], axis=-1)
    return jnp.tile(cos64, (1, 2)), jnp.tile(sin64, (1, 2))


def _attn_mixer(x, mod, layer, ng, cache_k, cache_v, w_qkv, qk_g, lam_vec, subln_g, w_out):
    lam_init = 0.8 - 0.6 * math.exp(-0.3 * layer)
    cos_t, sin_t = _rope_tables()
    qkg = jnp.tile(qk_g, (1, 2))
    lane = jnp.arange(LANES)
    gsum = (lane[:, None] // HEAD_DIM_C == lane[None, :] // HEAD_DIM_C).astype(BF16)
    n_prompt_q = N_PROMPT // TM_Q
    per_seq = DEC_SEQ // TM_Q
    prm = lambda i: (jnp.minimum(i, n_prompt_q - 1), 0)
    smp = lambda i: (jnp.maximum(i - n_prompt_q, 0), 0)
    rope_blk = lambda i: (i % per_seq, 0)
    q, kp, ks, vp, vs = pl.pallas_call(
        _qkv_kernel,
        grid=(N_ROWS // TM_Q,),
        in_specs=[
            pl.BlockSpec((TM_Q, D_MODEL), lambda i: (i, 0)),
            _mod_spec(layer, 0, TM_Q), _mod_spec(layer, 1, TM_Q),
            pl.BlockSpec((1, D_MODEL), lambda i: (0, 0)),
            pl.BlockSpec((D_MODEL, 3 * D_MODEL), lambda i: (0, 0), pipeline_mode=pl.Buffered(1)),
            pl.BlockSpec((2, LANES), lambda i: (0, 0)),
            pl.BlockSpec((LANES, LANES), lambda i: (0, 0)),
            pl.BlockSpec((TM_Q, LANES), rope_blk),
            pl.BlockSpec((TM_Q, LANES), rope_blk),
        ],
        out_specs=[
            pl.BlockSpec((TM_Q, D_MODEL), lambda i: (i, 0)),
            pl.BlockSpec((TM_Q, D_MODEL), prm), pl.BlockSpec((TM_Q, D_MODEL), smp),
            pl.BlockSpec((TM_Q, D_MODEL), prm), pl.BlockSpec((TM_Q, D_MODEL), smp),
        ],
        out_shape=[
            jax.ShapeDtypeStruct((N_ROWS, D_MODEL), BF16),
            jax.ShapeDtypeStruct((N_PROMPT, D_MODEL), F32), jax.ShapeDtypeStruct((N_SAMPLE, D_MODEL), F32),
            jax.ShapeDtypeStruct((N_PROMPT, D_MODEL), F32), jax.ShapeDtypeStruct((N_SAMPLE, D_MODEL), F32),
        ],
        compiler_params=_params(("arbitrary",)),
        name=f"qkv{layer}",
    )(x, mod, mod, ng.reshape(1, D_MODEL), w_qkv.astype(BF16), qkg, gsum, cos_t, sin_t)

    sg = subln_g.reshape(1, 2 * HEAD_DIM_C)
    o_p = pl.pallas_call(
        functools.partial(_attn_prompt_kernel, lam_init=lam_init),
        grid=(BATCH,),
        in_specs=[
            pl.BlockSpec((SEQ, D_MODEL), lambda b: (b, 0)),
            pl.BlockSpec((SEQ, D_MODEL), lambda b: (b, 0)),
            pl.BlockSpec((SEQ, D_MODEL), lambda b: (b, 0)),
            pl.BlockSpec((4, HEAD_DIM_C), lambda b: (0, 0)),
            pl.BlockSpec((1, 2 * HEAD_DIM_C), lambda b: (0, 0)),
        ],
        out_specs=pl.BlockSpec((SEQ, D_MODEL), lambda b: (b, 0)),
        out_shape=jax.ShapeDtypeStruct((N_PROMPT, D_MODEL), BF16),
        scratch_shapes=[pltpu.VMEM((SEQ, D_MODEL), BF16), pltpu.VMEM((SEQ, D_MODEL), BF16)],
        compiler_params=_params(("arbitrary",)),
        name=f"attn_prompt{layer}",
    )(q, kp, vp, lam_vec, sg)

    n_qb = DEC_SEQ // TQ
    t_all = PAST_LEN + DEC_SEQ
    o_s = pl.pallas_call(
        functools.partial(_attn_sample_kernel, lam_init=lam_init),
        grid=(DEC_BATCH, n_qb),
        in_specs=[
            pl.BlockSpec((TQ, D_MODEL), lambda b, t: (N_PROMPT // TQ + b * n_qb + t, 0)),
            pl.BlockSpec((DEC_SEQ, D_MODEL), lambda b, t: (b, 0)),
            pl.BlockSpec((DEC_SEQ, D_MODEL), lambda b, t: (b, 0)),
            pl.BlockSpec((None, PAST_LEN, D_MODEL), lambda b, t: (b, 0, 0)),
            pl.BlockSpec((None, PAST_LEN, D_MODEL), lambda b, t: (b, 0, 0)),
            pl.BlockSpec((4, HEAD_DIM_C), lambda b, t: (0, 0)),
            pl.BlockSpec((1, 2 * HEAD_DIM_C), lambda b, t: (0, 0)),
        ],
        out_specs=pl.BlockSpec((TQ, D_MODEL), lambda b, t: (b * n_qb + t, 0)),
        out_shape=jax.ShapeDtypeStruct((N_SAMPLE, D_MODEL), BF16),
        scratch_shapes=[pltpu.VMEM((t_all, D_MODEL), BF16), pltpu.VMEM((t_all, D_MODEL), BF16)],
        compiler_params=_params(("arbitrary", "arbitrary")),
        name=f"attn_sample{layer}",
    )(q, ks, vs, cache_k.reshape(DEC_BATCH, PAST_LEN, D_MODEL), cache_v.reshape(DEC_BATCH, PAST_LEN, D_MODEL),
      lam_vec, sg)

    x_new = pl.pallas_call(
        _attn_out_kernel,
        grid=(N_TILES,),
        in_specs=[
            pl.BlockSpec((TM, D_MODEL), lambda i: (i, 0)),
            _mod_spec(layer, 2),
            pl.BlockSpec((TM, D_MODEL), lambda i: (jnp.minimum(i, N_PROMPT_TILES - 1), 0)),
            pl.BlockSpec((TM, D_MODEL), lambda i: (jnp.maximum(i - N_PROMPT_TILES, 0), 0)),
            pl.BlockSpec((D_MODEL, D_MODEL), lambda i: (0, 0)),
        ],
        out_specs=pl.BlockSpec((TM, D_MODEL), lambda i: (i, 0)),
        out_shape=jax.ShapeDtypeStruct((N_ROWS, D_MODEL), F32),
        compiler_params=_params(("arbitrary",)),
        name=f"attn_out{layer}",
    )(x, mod, o_p, o_s, w_out)
    return x_new, kp, vp


def kernel(x_prompt, x_sample, state_lru, cache_k, cache_v, c, c_ctx, w_mod, b_mod, norm_g, lru_w_in, lru_conv_w, lru_conv_b, lru_w_gate, lru_b_gate, lru_lambda, lru_w_out, cmlp_w_in, cmlp_b_in, cmlp_norm_g, cmlp_w_s, cmlp_b_s, cmlp_w_out, attn_w_qkv, attn_qk_g, attn_lambda, attn_subln_g, attn_w_out, ffn_w_up, ffn_conv_w, ffn_conv_b, ffn_w_down):
    x = (x_prompt.reshape(N_PROMPT, D_MODEL), x_sample.reshape(N_SAMPLE, D_MODEL))
    cond8 = jnp.concatenate([c_ctx[None, :], c, jnp.zeros((SUBLANES - 1 - DEC_BATCH, D_MODEL), F32)], axis=0)
    mod = _adaln(cond8, w_mod, b_mod)
    norm_g4 = norm_g.reshape(DEPTH, 2, 1, D_MODEL)
    new_lru, new_k, new_v = [], [], []
    for l in range(DEPTH):
        kind, j = l % 3, l // 3
        if kind == 0:
            x, st = _lru_mixer(x, mod, l, j, norm_g4, state_lru, lru_w_in, lru_conv_w, lru_conv_b,
                               lru_w_gate, lru_b_gate, lru_lambda, lru_w_out)
            new_lru.append(st)
        elif kind == 1:
            x = _cmlp_mixer(x, mod, l, norm_g[l, 0], cmlp_w_in[j], cmlp_b_in[j], cmlp_norm_g[j],
                            cmlp_w_s[j], cmlp_b_s[j], cmlp_w_out[j])
        else:
            x, kp, vp = _attn_mixer(x, mod, l, norm_g[l, 0], cache_k[:, j], cache_v[:, j], attn_w_qkv[j],
                                    attn_qk_g[j], attn_lambda[j], attn_subln_g[j], attn_w_out[j])
            new_k.append(kp.reshape(BATCH, SEQ, N_HEADS_C, 2, HEAD_DIM_C))
            new_v.append(vp.reshape(BATCH, SEQ, N_HEADS_C, 2 * HEAD_DIM_C))
        x = _ffn(x, mod, l, norm_g4, ffn_w_up, ffn_conv_w, ffn_conv_b, ffn_w_down, split_out=(l == DEPTH - 1))
    y_prompt = x[0].reshape(BATCH, SEQ, D_MODEL)
    y_sample = x[1].reshape(DEC_BATCH, DEC_SEQ, D_MODEL)
    return (y_prompt, y_sample, jnp.stack(new_lru, axis=1), jnp.stack(new_k, axis=1), jnp.stack(new_v, axis=1))
```

```python
import functools
import math

import jax
import jax.numpy as jnp
from jax import lax
from jax.experimental import pallas as pl
from jax.experimental.pallas import tpu as pltpu

D_MODEL = 1024
BATCH = 32
SEQ = 256
DEPTH = 4
DEC_BATCH = 2
DEC_SEQ = 1024
PAST_LEN = 512
GRID_W = 64
N_MOD = 6
EPS = 1e-6
D_RNN = 1280
LRU_BLOCKS = 10
LRU_BLOCK = 128
LRU_CONV_W = 4
LRU_CONV_LEFT = 2
LRU_C = 8.0
D_B = 2 * D_MODEL
CHUNK = 128
G_B = 8
N_HEADS_C = 8
HEAD_DIM_C = 64
N_FREQ_AXIS = 16
ROPE_BASE = 10000.0
D_FF = 2816
FFN_CONV_W = 3

N_PROMPT = BATCH * SEQ
N_SAMPLE = DEC_BATCH * DEC_SEQ
N_ROWS = N_PROMPT + N_SAMPLE
TM = 1024
N_TILES = N_ROWS // TM
N_PROMPT_TILES = N_PROMPT // TM
SEQ_PER_TILE = TM // SEQ
SUBLANES = 8
LANES = 128
N_SEG = SUBLANES
SEG = TM // N_SEG
FF_TN = 256
FF_BLK = 128
LRU_TN = 256
N_FF_CHUNKS = D_FF // FF_TN
TM_B = 512
TM_Q = 512
TQ = 256
VMEM_LIMIT = 56 * 1024 * 1024

BF16 = jnp.bfloat16
F32 = jnp.float32


def _dot(a, b):
    return jnp.dot(a, b, preferred_element_type=F32)


def _dot_nt(a, b):
    return lax.dot_general(a, b, (((1,), (1,)), ((), ())), preferred_element_type=F32)


def _mod_row(i):
    return jnp.maximum(i - (N_PROMPT_TILES - 1), 0)


def _modulate(x, ng, sh, sc):
    ms = jnp.mean(x * x, axis=-1, keepdims=True)
    return (x * lax.rsqrt(ms + EPS) * ng) * (1.0 + sc) + sh


def _gelu(x):
    k = math.sqrt(2.0 / math.pi)
    return (0.5 * x) * (1.0 + jnp.tanh(x * (k + (k * 0.044715) * (x * x))))


def _sigmoid(x):
    return 0.5 * jnp.tanh(0.5 * x) + 0.5


def _params(sem):
    return pltpu.CompilerParams(dimension_semantics=sem, vmem_limit_bytes=VMEM_LIMIT)


def _adaln_kernel(cond_ref, w_ref, b_ref, o_ref):
    cnd = cond_ref[...]
    s = (cnd * _sigmoid(cnd)).astype(BF16)
    o_ref[...] = _dot(s, w_ref[...].astype(BF16)) + b_ref[...]


def _adaln(cond8, w_mod, b_mod):
    out = pl.pallas_call(
        _adaln_kernel,
        grid=(DEPTH, N_MOD),
        in_specs=[
            pl.BlockSpec((SUBLANES, D_MODEL), lambda l, k: (0, 0)),
            pl.BlockSpec((None, D_MODEL, D_MODEL), lambda l, k: (l, 0, k)),
            pl.BlockSpec((None, None, 1, D_MODEL), lambda l, k: (l, k, 0, 0)),
        ],
        out_specs=pl.BlockSpec((None, None, SUBLANES, D_MODEL), lambda l, k: (l, k, 0, 0)),
        out_shape=jax.ShapeDtypeStruct((DEPTH, N_MOD, SUBLANES, D_MODEL), F32),
        compiler_params=_params(("arbitrary", "arbitrary")),
        name="adaln",
    )(cond8, w_mod, b_mod.reshape(DEPTH, N_MOD, 1, D_MODEL))
    return out.reshape(DEPTH, N_MOD, SUBLANES, 1, D_MODEL)


def _mod_spec(layer, k, tile_rows=TM):
    per = TM // tile_rows
    return pl.BlockSpec((None, None, None, 1, D_MODEL),
                        lambda i, *_: (layer, k, _mod_row(i // per), 0, 0))


def _ng_spec(layer, which):
    return pl.BlockSpec((None, None, 1, D_MODEL), lambda i, *_: (layer, which, 0, 0))


def _modulate_seg_major(x_ref, ng_ref, sh_ref, sc_ref, perm_scr, h_scr):
    for r in range(N_SEG):
        rows = slice(r * SEG, (r + 1) * SEG)
        h = _modulate(x_ref[rows, :], ng_ref[...], sh_ref[...], sc_ref[...])
        for cb in range(D_MODEL // LANES):
            perm_scr[cb, pl.ds(r, SEG, stride=N_SEG), :] = h[:, cb * LANES:(cb + 1) * LANES]
    for cb in range(D_MODEL // LANES):
        h_scr[:, cb * LANES:(cb + 1) * LANES] = perm_scr[cb].astype(BF16)


def _residual_from_seg_major(x_ref, gt_ref, perm_scr, o_ref):
    for r in range(N_SEG):
        rows = slice(r * SEG, (r + 1) * SEG)
        for cb in range(D_MODEL // LANES):
            cols = slice(cb * LANES, (cb + 1) * LANES)
            y = perm_scr[cb, pl.ds(r, SEG, stride=N_SEG), :]
            o_ref[rows, cols] = x_ref[rows, cols] + gt_ref[:, cols] * y


def _seg_masks(i, width):
    per_seq = jnp.where(i < N_PROMPT_TILES, SEQ // SEG, DEC_SEQ // SEG)
    r = lax.broadcasted_iota(jnp.int32, (N_SEG, width), 0) & (per_seq - 1)
    return r == 0, r == per_seq - 1


def _shift_prev(z, is_start):
    wrap = jnp.where(is_start, 0.0, pltpu.roll(z[TM - N_SEG:, :], 1, 0))
    return jnp.concatenate([wrap, z[:TM - N_SEG, :]], axis=0)


def _shift_next(z, is_end):
    wrap = jnp.where(is_end, 0.0, pltpu.roll(z[:N_SEG, :], N_SEG - 1, 0))
    return jnp.concatenate([z[N_SEG:, :], wrap], axis=0)


def _ffn_kernel(x_ref, sh_ref, sc_ref, gt_ref, ng_ref, wg_ref, wu_ref, cwg_ref, cwu_ref,
                cbg_ref, cbu_ref, wd_ref, o_ref, h_scr, acc_scr, w_scr, z_scr, act_scr):
    i = pl.program_id(0)
    j = pl.program_id(1)

    @pl.when(j == 0)
    def _():
        _modulate_seg_major(x_ref, ng_ref, sh_ref, sc_ref, acc_scr, h_scr)
        acc_scr[...] = jnp.zeros_like(acc_scr)

    is_start, is_end = _seg_masks(i, 2 * FF_TN)
    w_scr[:, :FF_TN] = wg_ref[...].astype(BF16)
    w_scr[:, FF_TN:] = wu_ref[...].astype(BF16)
    cw = jnp.concatenate([cwg_ref[...], cwu_ref[...]], axis=1)
    cb = jnp.concatenate([cbg_ref[...], cbu_ref[...]], axis=1)

    def up(m):
        r0 = m * FF_BLK
        z_scr[N_SEG + r0:N_SEG + r0 + FF_BLK, :] = _dot(h_scr[r0:r0 + FF_BLK, :], w_scr[...])

    def post(m):
        r0 = N_SEG + m * FF_BLK
        c = (cb + z_scr[r0 - N_SEG:r0 - N_SEG + FF_BLK, :] * cw[0:1, :] + z_scr[r0:r0 + FF_BLK, :] * cw[1:2, :]
             + z_scr[r0 + N_SEG:r0 + N_SEG + FF_BLK, :] * cw[2:3, :])
        hg = 0.5 * c[:, :FF_TN]
        act = (hg * (jnp.tanh(hg) + 1.0)) * c[:, FF_TN:]
        act_scr[m * FF_BLK:(m + 1) * FF_BLK, :] = act.astype(BF16)

    n_blk = TM // FF_BLK
    up(n_blk - 1)
    z_scr[0:N_SEG, :] = jnp.where(is_start, 0.0, pltpu.roll(z_scr[TM:TM + N_SEG, :], 1, 0))
    up(0)
    z_scr[TM + N_SEG:, :] = jnp.where(is_end, 0.0, pltpu.roll(z_scr[N_SEG:2 * N_SEG, :], N_SEG - 1, 0))
    up(1)
    for m in range(n_blk):
        if m + 2 < n_blk - 1:
            up(m + 2)
        post(m)
    y = _dot(act_scr[...], wd_ref[...].astype(BF16))
    for cb_i in range(D_MODEL // LANES):
        acc_scr[cb_i] += y[:, cb_i * LANES:(cb_i + 1) * LANES]

    @pl.when(j == N_FF_CHUNKS - 1)
    def _():
        _residual_from_seg_major(x_ref, gt_ref, acc_scr, o_ref)


N_FFN_INPUTS = 12


def _ffn_kernel_split_out(*refs):
    ins = refs[:N_FFN_INPUTS]
    op_ref, os_ref = refs[N_FFN_INPUTS:N_FFN_INPUTS + 2]
    *scratch, o_scr = refs[N_FFN_INPUTS + 2:]
    _ffn_kernel(*ins, o_scr, *scratch)
    last = pl.program_id(1) == N_FF_CHUNKS - 1
    is_prompt = pl.program_id(0) < N_PROMPT_TILES

    @pl.when(last & is_prompt)
    def _():
        op_ref[...] = o_scr[...]

    @pl.when(last & jnp.logical_not(is_prompt))
    def _():
        os_ref[...] = o_scr[...]


def _ffn(x, mod, layer, norm_g4, w_up, conv_w, conv_b, w_down, split_out=False):
    conv_b3 = conv_b.reshape(DEPTH, 1, 2 * D_FF)
    scratch = [pltpu.VMEM((TM, D_MODEL), BF16), pltpu.VMEM((D_MODEL // LANES, TM, LANES), F32),
               pltpu.VMEM((D_MODEL, 2 * FF_TN), BF16), pltpu.VMEM((TM + 2 * N_SEG, 2 * FF_TN), F32),
               pltpu.VMEM((TM, FF_TN), BF16)]
    if split_out:
        body = _ffn_kernel_split_out
        out_specs = [pl.BlockSpec((TM, D_MODEL), lambda i, j: (jnp.minimum(i, N_PROMPT_TILES - 1), 0)),
                     pl.BlockSpec((TM, D_MODEL), lambda i, j: (jnp.maximum(i - N_PROMPT_TILES, 0), 0))]
        out_shape = [jax.ShapeDtypeStruct((N_PROMPT, D_MODEL), F32), jax.ShapeDtypeStruct((N_SAMPLE, D_MODEL), F32)]
        scratch = scratch + [pltpu.VMEM((TM, D_MODEL), F32)]
    else:
        body = _ffn_kernel
        out_specs = pl.BlockSpec((TM, D_MODEL), lambda i, j: (i, 0))
        out_shape = jax.ShapeDtypeStruct((N_ROWS, D_MODEL), F32)
    return pl.pallas_call(
        body,
        grid=(N_TILES, N_FF_CHUNKS),
        in_specs=[
            pl.BlockSpec((TM, D_MODEL), lambda i, j: (i, 0)),
            _mod_spec(layer, 3), _mod_spec(layer, 4), _mod_spec(layer, 5),
            _ng_spec(layer, 1),
            pl.BlockSpec((None, D_MODEL, FF_TN), lambda i, j: (layer, 0, j)),
            pl.BlockSpec((None, D_MODEL, FF_TN), lambda i, j: (layer, 0, N_FF_CHUNKS + j)),
            pl.BlockSpec((None, FFN_CONV_W, FF_TN), lambda i, j: (layer, 0, j)),
            pl.BlockSpec((None, FFN_CONV_W, FF_TN), lambda i, j: (layer, 0, N_FF_CHUNKS + j)),
            pl.BlockSpec((None, 1, FF_TN), lambda i, j: (layer, 0, j)),
            pl.BlockSpec((None, 1, FF_TN), lambda i, j: (layer, 0, N_FF_CHUNKS + j)),
            pl.BlockSpec((None, FF_TN, D_MODEL), lambda i, j: (layer, j, 0)),
        ],
        out_specs=out_specs,
        out_shape=out_shape,
        scratch_shapes=scratch,
        compiler_params=_params(("arbitrary", "arbitrary")),
        name=f"ffn{layer}",
    )(x, mod, mod, mod, norm_g4, w_up, w_up, conv_w, conv_w, conv_b3, conv_b3, w_down)


def _lru_kernel(x_ref, sh_ref, sc_ref, gt_ref, ng_ref, wig_ref, wir_ref, cw_ref, cb_ref, wgt_ref, bgt_ref,
                lam_ref, h0_ref, wout_ref, o_ref, st_ref,
                h_scr, yg_scr, perm_scr, gg_scr, af_scr, bf_scr, ab_scr, bb_scr):
    i = pl.program_id(0)
    n = pl.program_id(1)

    @pl.when(n == 0)
    def _():
        _modulate_seg_major(x_ref, ng_ref, sh_ref, sc_ref, perm_scr, h_scr)

    h = h_scr[...]
    is_start, is_end = _seg_masks(i, LRU_TN)
    gg_scr[...] = _gelu(_dot(h, wig_ref[...].astype(BF16)))
    rec = _dot(h, wir_ref[...].astype(BF16))
    prev1 = _shift_prev(rec, is_start)
    prev2 = _shift_prev(prev1, is_start)
    xc = (cb_ref[...] + prev2 * cw_ref[0:1, :] + prev1 * cw_ref[1:2, :] + rec * cw_ref[2:3, :]
          + _shift_next(rec, is_end) * cw_ref[3:4, :])

    lam = lam_ref[...]
    softplus = jnp.maximum(-lam, 0.0) + jnp.log1p(jnp.exp(-jnp.abs(lam)))
    for e in range(LRU_TN // LRU_BLOCK):
        cols = slice(e * LRU_BLOCK, (e + 1) * LRU_BLOCK)
        xce = xc[:, cols]
        w4 = jnp.concatenate([wgt_ref[d, g, e] for d in range(2) for g in range(2)], axis=1)
        b4 = jnp.concatenate([bgt_ref[d, g, e] for d in range(2) for g in range(2)], axis=1)
        gates = _sigmoid(_dot(xce.astype(BF16), w4.astype(BF16)) + b4)
        for d, (a_scr, b_scr) in enumerate(((af_scr, bf_scr), (ab_scr, bb_scr))):
            r = gates[:, (2 * d) * LRU_BLOCK:(2 * d + 1) * LRU_BLOCK]
            ig = gates[:, (2 * d + 1) * LRU_BLOCK:(2 * d + 2) * LRU_BLOCK]
            log_a = (-LRU_C * r) * softplus[d:d + 1, cols]
            a = jnp.exp(log_a)
            a_scr[:, cols] = a
            b_scr[:, cols] = jnp.sqrt(jnp.maximum(1.0 - a * a, 0.0)) * (ig * xce)

    def put(k, row, v):
        perm_scr[2 * k, pl.ds(row, N_SEG), :] = v[:, :LANES]
        perm_scr[2 * k + 1, pl.ds(row, N_SEG), :] = v[:, LANES:]

    def get(k):
        return jnp.concatenate([perm_scr[2 * k], perm_scr[2 * k + 1]], axis=1)

    def local_scan(t, carry):
        hf, cf, hb, cb = carry
        rf = pl.multiple_of(t * N_SEG, N_SEG)
        rb = pl.multiple_of((SEG - 1 - t) * N_SEG, N_SEG)
        a = af_scr[pl.ds(rf, N_SEG), :]
        hf = a * hf + bf_scr[pl.ds(rf, N_SEG), :]
        cf = a * cf
        put(0, rf, hf)
        put(1, rf, cf)
        a = ab_scr[pl.ds(rb, N_SEG), :]
        hb = a * hb + bb_scr[pl.ds(rb, N_SEG), :]
        cb = a * cb
        put(2, rb, hb)
        put(3, rb, cb)
        return hf, cf, hb, cb

    zeros = jnp.zeros((N_SEG, LRU_TN), F32)
    ones = jnp.ones((N_SEG, LRU_TN), F32)
    end_f, dec_f, end_b, dec_b = lax.fori_loop(0, SEG, local_scan, (zeros, ones, zeros, ones), unroll=8)

    per_seq = jnp.where(i < N_PROMPT_TILES, SEQ // SEG, DEC_SEQ // SEG)
    sub = lax.broadcasted_iota(jnp.int32, (N_SEG, LRU_TN), 0)
    h0f = h0_ref[0:1, :]
    h0b = h0_ref[1:2, :]
    in_f, in_b = zeros, zeros
    out_f, out_b = [None] * N_SEG, [None] * N_SEG
    for r in range(N_SEG):
        hin = h0f if r == 0 else jnp.where((r & (per_seq - 1)) == 0, h0f, out_f[r - 1])
        in_f = jnp.where(sub == r, hin, in_f)
        out_f[r] = end_f[r:r + 1, :] + dec_f[r:r + 1, :] * hin
    for r in reversed(range(N_SEG)):
        hin = h0b if r == N_SEG - 1 else jnp.where((r & (per_seq - 1)) == per_seq - 1, h0b, out_b[r + 1])
        in_b = jnp.where(sub == r, hin, in_b)
        out_b[r] = end_b[r:r + 1, :] + dec_b[r:r + 1, :] * hin
    for q in range(SEQ_PER_TILE):
        st_ref[q, 0:1, :] = out_f[2 * q + 1]
        st_ref[q, 1:2, :] = out_b[2 * q]

    def every_step(v):
        return jnp.broadcast_to(v[None], (SEG, N_SEG, LRU_TN)).reshape(TM, LRU_TN)

    hsum = ((get(0) + get(1) * every_step(in_f))
            + (get(2) + get(3) * every_step(in_b)))
    col = pl.multiple_of(n * LRU_TN, LRU_TN)
    yg_scr[:, pl.ds(col, LRU_TN)] = (gg_scr[...] * hsum).astype(BF16)

    @pl.when(n == D_RNN // LRU_TN - 1)
    def _():
        y = _dot(yg_scr[...], wout_ref[...].astype(BF16))
        for cb in range(D_MODEL // LANES):
            perm_scr[cb] = y[:, cb * LANES:(cb + 1) * LANES]
        _residual_from_seg_major(x_ref, gt_ref, perm_scr, o_ref)


def _lru_kernel_split_in(xp_ref, xs_ref, *refs):
    *rest, x_scr = refs
    first = pl.program_id(1) == 0
    is_prompt = pl.program_id(0) < N_PROMPT_TILES

    @pl.when(first & is_prompt)
    def _():
        x_scr[...] = xp_ref[...]

    @pl.when(first & jnp.logical_not(is_prompt))
    def _():
        x_scr[...] = xs_ref[...]

    _lru_kernel(x_scr, *rest)


def _lru_mixer(x, mod, layer, j, norm_g4, state_lru, w_in, conv_w, conv_b, w_gate, b_gate, lam, w_out):
    n_steps = D_RNN // LRU_TN
    per_step = LRU_TN // LRU_BLOCK
    n_lru = w_in.shape[0]
    h0 = jnp.concatenate([jnp.zeros((N_PROMPT_TILES, 2, D_RNN), F32), state_lru[:, j]], axis=0)
    scratch = ([pltpu.VMEM((TM, D_MODEL), BF16), pltpu.VMEM((TM, D_RNN), BF16),
                pltpu.VMEM((D_MODEL // LANES, TM, LANES), F32)] + [pltpu.VMEM((TM, LRU_TN), F32)] * 5)
    if isinstance(x, tuple):
        body = _lru_kernel_split_in
        x_args = x
        x_specs = [pl.BlockSpec((TM, D_MODEL), lambda i, n: (jnp.minimum(i, N_PROMPT_TILES - 1), 0)),
                   pl.BlockSpec((TM, D_MODEL), lambda i, n: (jnp.maximum(i - N_PROMPT_TILES, 0), 0))]
        scratch = scratch + [pltpu.VMEM((TM, D_MODEL), F32)]
    else:
        body = _lru_kernel
        x_args = (x,)
        x_specs = [pl.BlockSpec((TM, D_MODEL), lambda i, n: (i, 0))]
    out, st = pl.pallas_call(
        body,
        grid=(N_TILES, n_steps),
        in_specs=x_specs + [
            _mod_spec(layer, 0), _mod_spec(layer, 1), _mod_spec(layer, 2),
            _ng_spec(layer, 0),
            pl.BlockSpec((None, D_MODEL, LRU_TN), lambda i, n: (j, 0, n)),
            pl.BlockSpec((None, D_MODEL, LRU_TN), lambda i, n: (j, 0, n_steps + n)),
            pl.BlockSpec((None, LRU_CONV_W, LRU_TN), lambda i, n: (j, 0, n)),
            pl.BlockSpec((None, 1, LRU_TN), lambda i, n: (j, 0, n)),
            pl.BlockSpec((None, 2, 2, per_step, LRU_BLOCK, LRU_BLOCK), lambda i, n: (j, 0, 0, n, 0, 0)),
            pl.BlockSpec((None, 2, 2, per_step, 1, LRU_BLOCK), lambda i, n: (j, 0, 0, n, 0, 0)),
            pl.BlockSpec((None, 2, LRU_TN), lambda i, n: (j, 0, n)),
            pl.BlockSpec((None, 2, LRU_TN), lambda i, n: (i, 0, n)),
            pl.BlockSpec((None, D_RNN, D_MODEL), lambda i, n: (j, 0, 0), pipeline_mode=pl.Buffered(1)),
        ],
        out_specs=[
            pl.BlockSpec((TM, D_MODEL), lambda i, n: (i, 0)),
            pl.BlockSpec((SEQ_PER_TILE, 2, LRU_TN), lambda i, n: (i, 0, n)),
        ],
        out_shape=[
            jax.ShapeDtypeStruct((N_ROWS, D_MODEL), F32),
            jax.ShapeDtypeStruct((N_TILES * SEQ_PER_TILE, 2, D_RNN), F32),
        ],
        scratch_shapes=scratch,
        compiler_params=_params(("arbitrary", "arbitrary")),
        name=f"lru{layer}",
    )(*x_args, mod, mod, mod, norm_g4, w_in, w_in, conv_w, conv_b.reshape(n_lru, 1, D_RNN), w_gate,
      b_gate.reshape(n_lru, 2, 2, LRU_BLOCKS, 1, LRU_BLOCK), lam, h0, w_out)
    return out, st[:BATCH]


def _cmlp_kernel(x_ref, sh_ref, sc_ref, gt_ref, ng_ref, win_ref, bin_ref, vng_ref, ws_ref, bs_ref,
                 wout_ref, o_ref, u_scr, v_scr, uv_scr):
    gw = D_B // G_B
    h = _modulate(x_ref[...], ng_ref[...], sh_ref[...], sc_ref[...]).astype(BF16)
    ssq = jnp.zeros((TM_B, 1), F32)
    for g in range(G_B):
        cu = slice(g * gw, (g + 1) * gw)
        cv = slice(D_B + g * gw, D_B + (g + 1) * gw)
        u_scr[:, cu] = _gelu(_dot(h, win_ref[:, cu]) + bin_ref[:, cu])
        v = _gelu(_dot(h, win_ref[:, cv]) + bin_ref[:, cv])
        v_scr[:, cu] = v
        ssq = ssq + jnp.sum(v * v, axis=-1, keepdims=True)
    rinv = lax.rsqrt(ssq * (1.0 / D_B) + EPS)
    for g in range(G_B):
        cu = slice(g * gw, (g + 1) * gw)
        vn = ((v_scr[:, cu] * rinv) * vng_ref[:, cu]).astype(BF16)
        w_s = ws_ref[g]
        for c in range(TM_B // CHUNK):
            rows = slice(c * CHUNK, (c + 1) * CHUNK)
            sv = _dot(w_s, vn[rows, :]) + bs_ref[:, cu]
            uv_scr[rows, cu] = (u_scr[rows, cu] * sv).astype(BF16)
    o_ref[...] = x_ref[...] + gt_ref[...] * _dot(uv_scr[...], wout_ref[...])


def _cmlp_mixer(x, mod, layer, ng, w_in, b_in, vnorm_g, w_s, b_s, w_out):
    gw = D_B // G_B
    bs_full = jnp.repeat(b_s.T, gw, axis=1)
    const = lambda *shape: pl.BlockSpec(shape, lambda i: (0,) * len(shape), pipeline_mode=pl.Buffered(1))
    return pl.pallas_call(
        _cmlp_kernel,
        grid=(N_ROWS // TM_B,),
        in_specs=[
            pl.BlockSpec((TM_B, D_MODEL), lambda i: (i, 0)),
            _mod_spec(layer, 0, TM_B), _mod_spec(layer, 1, TM_B), _mod_spec(layer, 2, TM_B),
            const(1, D_MODEL),
            const(D_MODEL, 2 * D_B),
            const(1, 2 * D_B),
            const(1, D_B),
            const(G_B, CHUNK, CHUNK),
            const(CHUNK, D_B),
            const(D_B, D_MODEL),
        ],
        out_specs=pl.BlockSpec((TM_B, D_MODEL), lambda i: (i, 0)),
        out_shape=jax.ShapeDtypeStruct((N_ROWS, D_MODEL), F32),
        scratch_shapes=[pltpu.VMEM((TM_B, D_B), F32), pltpu.VMEM((TM_B, D_B), F32),
                        pltpu.VMEM((TM_B, D_B), BF16)],
        compiler_params=_params(("arbitrary",)),
        name=f"cmlp{layer}",
    )(x, mod, mod, mod, ng.reshape(1, D_MODEL), w_in.astype(BF16), b_in.reshape(1, 2 * D_B),
      vnorm_g.reshape(1, D_B), w_s.astype(BF16), bs_full, w_out.astype(BF16))


def _qkv_kernel(x_ref, sh_ref, sc_ref, ng_ref, w_ref, qkg_ref, gsum_ref, cos_ref, sin_ref,
                q_ref, kp_ref, ks_ref, vp_ref, vs_ref):
    i = pl.program_id(0)
    h = _modulate(x_ref[...], ng_ref[...], sh_ref[...], sc_ref[...]).astype(BF16)
    lane = lax.broadcasted_iota(jnp.int32, (TM_Q, LANES), 1)
    first_half = (lane & (HEAD_DIM_C - 1)) < HEAD_DIM_C // 2

    def norm_store(part, out_ref, rope, out_scale):
        z = _dot(h, w_ref[:, part * D_MODEL:(part + 1) * D_MODEL])
        for hb in range(N_HEADS_C):
            zb = z[:, hb * LANES:(hb + 1) * LANES]
            sq = zb * zb
            hi = sq.astype(BF16)
            lo = (sq - hi.astype(F32)).astype(BF16)
            ssum = _dot(hi, gsum_ref[...]) + _dot(lo, gsum_ref[...])
            zb = zb * lax.rsqrt(ssum * (1.0 / HEAD_DIM_C) + EPS) * qkg_ref[part:part + 1, :]
            if rope:
                partner = jnp.where(first_half, pltpu.roll(zb, LANES - HEAD_DIM_C // 2, 1),
                                    pltpu.roll(zb, HEAD_DIM_C // 2, 1))
                zb = zb * cos_ref[...] + partner * sin_ref[...]
            if out_scale != 1.0:
                zb = zb * out_scale
            out_ref[:, hb * LANES:(hb + 1) * LANES] = zb.astype(out_ref.dtype)

    def qkv(rope, k_ref, v_ref):
        norm_store(0, q_ref, rope, HEAD_DIM_C ** -0.5)
        norm_store(1, k_ref, rope, 1.0)
        v_ref[...] = _dot(h, w_ref[:, 2 * D_MODEL:])

    @pl.when(i < N_PROMPT // TM_Q)
    def _():
        qkv(False, kp_ref, vp_ref)

    @pl.when(i >= N_PROMPT // TM_Q)
    def _():
        qkv(True, ks_ref, vs_ref)


def _attn_heads(q_ref, k_ref, v_ref, lam_ref, sg_ref, o_ref, lam_init):
    lv = lam_ref[...]
    lam = (jnp.exp(jnp.sum(lv[0:1, :] * lv[1:2, :], axis=-1, keepdims=True))
           - jnp.exp(jnp.sum(lv[2:3, :] * lv[3:4, :], axis=-1, keepdims=True)) + lam_init)
    lane = lax.broadcasted_iota(jnp.int32, (TQ, LANES), 1)
    comp0 = lane < HEAD_DIM_C
    for hd in range(N_HEADS_C):
        cols = slice(hd * LANES, (hd + 1) * LANES)
        qh = q_ref[:, cols]
        kh = k_ref[:, cols]
        vh = v_ref[:, cols]
        zero = jnp.zeros_like(qh)
        s0 = _dot_nt(jnp.where(comp0, qh, zero), kh)
        s1 = _dot_nt(jnp.where(comp0, zero, qh), kh)
        e0 = jnp.exp(s0 - jnp.max(s0, axis=-1, keepdims=True))
        e1 = jnp.exp(s1 - jnp.max(s1, axis=-1, keepdims=True))
        p0 = e0 / jnp.sum(e0, axis=-1, keepdims=True)
        p1 = e1 / jnp.sum(e1, axis=-1, keepdims=True)
        w = p0 - lam * p1
        o = _dot(w.astype(BF16), vh)
        o = o * lax.rsqrt(jnp.mean(o * o, axis=-1, keepdims=True) + EPS) * sg_ref[...]
        o_ref[:, cols] = (o * (1.0 - lam_init)).astype(BF16)


def _attn_prompt_kernel(q_ref, k_ref, v_ref, lam_ref, sg_ref, o_ref, k_scr, v_scr, *, lam_init):
    k_scr[...] = k_ref[...].astype(BF16)
    v_scr[...] = v_ref[...].astype(BF16)
    _attn_heads(q_ref, k_scr, v_scr, lam_ref, sg_ref, o_ref, lam_init)


def _attn_sample_kernel(q_ref, kl_ref, vl_ref, kc_ref, vc_ref, lam_ref, sg_ref, o_ref, k_scr, v_scr,
                        *, lam_init):
    @pl.when(pl.program_id(1) == 0)
    def _():
        k_scr[0:PAST_LEN, :] = kc_ref[...].astype(BF16)
        k_scr[PAST_LEN:, :] = kl_ref[...].astype(BF16)
        v_scr[0:PAST_LEN, :] = vc_ref[...].astype(BF16)
        v_scr[PAST_LEN:, :] = vl_ref[...].astype(BF16)

    _attn_heads(q_ref, k_scr, v_scr, lam_ref, sg_ref, o_ref, lam_init)


def _attn_out_kernel(x_ref, gt_ref, op_ref, os_ref, w_ref, o_ref):
    i = pl.program_id(0)
    w = w_ref[...].astype(BF16)

    @pl.when(i < N_PROMPT_TILES)
    def _():
        o_ref[...] = x_ref[...] + gt_ref[...] * _dot(op_ref[...], w)

    @pl.when(i >= N_PROMPT_TILES)
    def _():
        o_ref[...] = x_ref[...] + gt_ref[...] * _dot(os_ref[...], w)


def _rope_tables():
    t = jnp.arange(DEC_SEQ)
    inv = ROPE_BASE ** (-jnp.arange(N_FREQ_AXIS, dtype=F32) / N_FREQ_AXIS)
    ang = jnp.concatenate([(t // GRID_W)[:, None] * inv, (t % GRID_W)[:, None] * inv], axis=-1)
    cos, sin = jnp.cos(ang), jnp.sin(ang)
    cos64 = jnp.concatenate([cos, cos], axis=-1)
    sin64 = jnp.concatenate([-sin, sin], axis=-1)
    return jnp.tile(cos64, (1, 2)), jnp.tile(sin64, (1, 2))


def _attn_mixer(x, mod, layer, ng, cache_k, cache_v, w_qkv, qk_g, lam_vec, subln_g, w_out):
    lam_init = 0.8 - 0.6 * math.exp(-0.3 * layer)
    cos_t, sin_t = _rope_tables()
    qkg = jnp.tile(qk_g, (1, 2))
    lane = jnp.arange(LANES)
    gsum = (lane[:, None] // HEAD_DIM_C == lane[None, :] // HEAD_DIM_C).astype(BF16)
    n_prompt_q = N_PROMPT // TM_Q
    per_seq = DEC_SEQ // TM_Q
    prm = lambda i: (jnp.minimum(i, n_prompt_q - 1), 0)
    smp = lambda i: (jnp.maximum(i - n_prompt_q, 0), 0)
    rope_blk = lambda i: (i % per_seq, 0)
    q, kp, ks, vp, vs = pl.pallas_call(
        _qkv_kernel,
        grid=(N_ROWS // TM_Q,),
        in_specs=[
            pl.BlockSpec((TM_Q, D_MODEL), lambda i: (i, 0)),
            _mod_spec(layer, 0, TM_Q), _mod_spec(layer, 1, TM_Q),
            pl.BlockSpec((1, D_MODEL), lambda i: (0, 0)),
            pl.BlockSpec((D_MODEL, 3 * D_MODEL), lambda i: (0, 0), pipeline_mode=pl.Buffered(1)),
            pl.BlockSpec((2, LANES), lambda i: (0, 0)),
            pl.BlockSpec((LANES, LANES), lambda i: (0, 0)),
            pl.BlockSpec((TM_Q, LANES), rope_blk),
            pl.BlockSpec((TM_Q, LANES), rope_blk),
        ],
        out_specs=[
            pl.BlockSpec((TM_Q, D_MODEL), lambda i: (i, 0)),
            pl.BlockSpec((TM_Q, D_MODEL), prm), pl.BlockSpec((TM_Q, D_MODEL), smp),
            pl.BlockSpec((TM_Q, D_MODEL), prm), pl.BlockSpec((TM_Q, D_MODEL), smp),
        ],
        out_shape=[
            jax.ShapeDtypeStruct((N_ROWS, D_MODEL), BF16),
            jax.ShapeDtypeStruct((N_PROMPT, D_MODEL), F32), jax.ShapeDtypeStruct((N_SAMPLE, D_MODEL), F32),
            jax.ShapeDtypeStruct((N_PROMPT, D_MODEL), F32), jax.ShapeDtypeStruct((N_SAMPLE, D_MODEL), F32),
        ],
        compiler_params=_params(("arbitrary",)),
        name=f"qkv{layer}",
    )(x, mod, mod, ng.reshape(1, D_MODEL), w_qkv.astype(BF16), qkg, gsum, cos_t, sin_t)

    sg = subln_g.reshape(1, 2 * HEAD_DIM_C)
    o_p = pl.pallas_call(
        functools.partial(_attn_prompt_kernel, lam_init=lam_init),
        grid=(BATCH,),
        in_specs=[
            pl.BlockSpec((SEQ, D_MODEL), lambda b: (b, 0)),
            pl.BlockSpec((SEQ, D_MODEL), lambda b: (b, 0)),
            pl.BlockSpec((SEQ, D_MODEL), lambda b: (b, 0)),
            pl.BlockSpec((4, HEAD_DIM_C), lambda b: (0, 0)),
            pl.BlockSpec((1, 2 * HEAD_DIM_C), lambda b: (0, 0)),
        ],
        out_specs=pl.BlockSpec((SEQ, D_MODEL), lambda b: (b, 0)),
        out_shape=jax.ShapeDtypeStruct((N_PROMPT, D_MODEL), BF16),
        scratch_shapes=[pltpu.VMEM((SEQ, D_MODEL), BF16), pltpu.VMEM((SEQ, D_MODEL), BF16)],
        compiler_params=_params(("arbitrary",)),
        name=f"attn_prompt{layer}",
    )(q, kp, vp, lam_vec, sg)

    n_qb = DEC_SEQ // TQ
    t_all = PAST_LEN + DEC_SEQ
    o_s = pl.pallas_call(
        functools.partial(_attn_sample_kernel, lam_init=lam_init),
        grid=(DEC_BATCH, n_qb),
        in_specs=[
            pl.BlockSpec((TQ, D_MODEL), lambda b, t: (N_PROMPT // TQ + b * n_qb + t, 0)),
            pl.BlockSpec((DEC_SEQ, D_MODEL), lambda b, t: (b, 0)),
            pl.BlockSpec((DEC_SEQ, D_MODEL), lambda b, t: (b, 0)),
            pl.BlockSpec((None, PAST_LEN, D_MODEL), lambda b, t: (b, 0, 0)),
            pl.BlockSpec((None, PAST_LEN, D_MODEL), lambda b, t: (b, 0, 0)),
            pl.BlockSpec((4, HEAD_DIM_C), lambda b, t: (0, 0)),
            pl.BlockSpec((1, 2 * HEAD_DIM_C), lambda b, t: (0, 0)),
        ],
        out_specs=pl.BlockSpec((TQ, D_MODEL), lambda b, t: (b * n_qb + t, 0)),
        out_shape=jax.ShapeDtypeStruct((N_SAMPLE, D_MODEL), BF16),
        scratch_shapes=[pltpu.VMEM((t_all, D_MODEL), BF16), pltpu.VMEM((t_all, D_MODEL), BF16)],
        compiler_params=_params(("arbitrary", "arbitrary")),
        name=f"attn_sample{layer}",
    )(q, ks, vs, cache_k.reshape(DEC_BATCH, PAST_LEN, D_MODEL), cache_v.reshape(DEC_BATCH, PAST_LEN, D_MODEL),
      lam_vec, sg)

    x_new = pl.pallas_call(
        _attn_out_kernel,
        grid=(N_TILES,),
        in_specs=[
            pl.BlockSpec((TM, D_MODEL), lambda i: (i, 0)),
            _mod_spec(layer, 2),
            pl.BlockSpec((TM, D_MODEL), lambda i: (jnp.minimum(i, N_PROMPT_TILES - 1), 0)),
            pl.BlockSpec((TM, D_MODEL), lambda i: (jnp.maximum(i - N_PROMPT_TILES, 0), 0)),
            pl.BlockSpec((D_MODEL, D_MODEL), lambda i: (0, 0)),
        ],
        out_specs=pl.BlockSpec((TM, D_MODEL), lambda i: (i, 0)),
        out_shape=jax.ShapeDtypeStruct((N_ROWS, D_MODEL), F32),
        compiler_params=_params(("arbitrary",)),
        name=f"attn_out{layer}",
    )(x, mod, o_p, o_s, w_out)
    return x_new, kp, vp


def kernel(x_prompt, x_sample, state_lru, cache_k, cache_v, c, c_ctx, w_mod, b_mod, norm_g, lru_w_in, lru_conv_w, lru_conv_b, lru_w_gate, lru_b_gate, lru_lambda, lru_w_out, cmlp_w_in, cmlp_b_in, cmlp_norm_g, cmlp_w_s, cmlp_b_s, cmlp_w_out, attn_w_qkv, attn_qk_g, attn_lambda, attn_subln_g, attn_w_out, ffn_w_up, ffn_conv_w, ffn_conv_b, ffn_w_down):
    x = (x_prompt.reshape(N_PROMPT, D_MODEL), x_sample.reshape(N_SAMPLE, D_MODEL))
    cond8 = jnp.concatenate([c_ctx[None, :], c, jnp.zeros((SUBLANES - 1 - DEC_BATCH, D_MODEL), F32)], axis=0)
    mod = _adaln(cond8, w_mod, b_mod)
    norm_g4 = norm_g.reshape(DEPTH, 2, 1, D_MODEL)
    new_lru, new_k, new_v = [], [], []
    for l in range(DEPTH):
        kind, j = l % 3, l // 3
        if kind == 0:
            x, st = _lru_mixer(x, mod, l, j, norm_g4, state_lru, lru_w_in, lru_conv_w, lru_conv_b,
                               lru_w_gate, lru_b_gate, lru_lambda, lru_w_out)
            new_lru.append(st)
        elif kind == 1:
            x = _cmlp_mixer(x, mod, l, norm_g[l, 0], cmlp_w_in[j], cmlp_b_in[j], cmlp_norm_g[j],
                            cmlp_w_s[j], cmlp_b_s[j], cmlp_w_out[j])
        else:
            x, kp, vp = _attn_mixer(x, mod, l, norm_g[l, 0], cache_k[:, j], cache_v[:, j], attn_w_qkv[j],
                                    attn_qk_g[j], attn_lambda[j], attn_subln_g[j], attn_w_out[j])
            new_k.append(kp.reshape(BATCH, SEQ, N_HEADS_C, 2, HEAD_DIM_C))
            new_v.append(vp.reshape(BATCH, SEQ, N_HEADS_C, 2 * HEAD_DIM_C))
        x = _ffn(x, mod, l, norm_g4, ffn_w_up, ffn_conv_w, ffn_conv_b, ffn_w_down, split_out=(l == DEPTH - 1))
    y_prompt = x[0].reshape(BATCH, SEQ, D_MODEL)
    y_sample = x[1].reshape(DEC_BATCH, DEC_SEQ, D_MODEL)
    return (y_prompt, y_sample, jnp.stack(new_lru, axis=1), jnp.stack(new_k, axis=1), jnp.stack(new_v, axis=1))
```

```python
import functools
import math

import jax
import jax.numpy as jnp
from jax import lax
from jax.experimental import pallas as pl
from jax.experimental.pallas import tpu as pltpu

D_MODEL = 1024
BATCH = 32
SEQ = 256
DEPTH = 4
DEC_BATCH = 2
DEC_SEQ = 1024
PAST_LEN = 512
GRID_W = 64
N_MOD = 6
EPS = 1e-6
D_RNN = 1280
LRU_BLOCKS = 10
LRU_BLOCK = 128
LRU_CONV_W = 4
LRU_CONV_LEFT = 2
LRU_C = 8.0
D_B = 2 * D_MODEL
CHUNK = 128
G_B = 8
N_HEADS_C = 8
HEAD_DIM_C = 64
N_FREQ_AXIS = 16
ROPE_BASE = 10000.0
D_FF = 2816
FFN_CONV_W = 3

N_PROMPT = BATCH * SEQ
N_SAMPLE = DEC_BATCH * DEC_SEQ
N_ROWS = N_PROMPT + N_SAMPLE
TM = 1024
N_TILES = N_ROWS // TM
N_PROMPT_TILES = N_PROMPT // TM
SEQ_PER_TILE = TM // SEQ
SUBLANES = 8
LANES = 128
N_SEG = SUBLANES
SEG = TM // N_SEG
FF_TN = 256
FF_BLK = 128
LRU_TN = 256
N_FF_CHUNKS = D_FF // FF_TN
TM_B = 512
TM_Q = 512
TQ = 256
VMEM_LIMIT = 56 * 1024 * 1024

BF16 = jnp.bfloat16
F32 = jnp.float32


def _dot(a, b):
    return jnp.dot(a, b, preferred_element_type=F32)


def _dot_nt(a, b):
    return lax.dot_general(a, b, (((1,), (1,)), ((), ())), preferred_element_type=F32)


def _mod_row(i):
    return jnp.maximum(i - (N_PROMPT_TILES - 1), 0)


def _modulate(x, ng, sh, sc):
    ms = jnp.mean(x * x, axis=-1, keepdims=True)
    return (x * lax.rsqrt(ms + EPS) * ng) * (1.0 + sc) + sh


def _gelu(x):
    k = math.sqrt(2.0 / math.pi)
    return (0.5 * x) * (1.0 + jnp.tanh(x * (k + (k * 0.044715) * (x * x))))


def _sigmoid(x):
    return 0.5 * jnp.tanh(0.5 * x) + 0.5


def _params(sem):
    return pltpu.CompilerParams(dimension_semantics=sem, vmem_limit_bytes=VMEM_LIMIT)


def _adaln_kernel(cond_ref, w_ref, b_ref, o_ref):
    cnd = cond_ref[...]
    s = (cnd * _sigmoid(cnd)).astype(BF16)
    o_ref[...] = _dot(s, w_ref[...].astype(BF16)) + b_ref[...]


def _adaln(cond8, w_mod, b_mod):
    out = pl.pallas_call(
        _adaln_kernel,
        grid=(DEPTH, N_MOD),
        in_specs=[
            pl.BlockSpec((SUBLANES, D_MODEL), lambda l, k: (0, 0)),
            pl.BlockSpec((None, D_MODEL, D_MODEL), lambda l, k: (l, 0, k)),
            pl.BlockSpec((None, None, 1, D_MODEL), lambda l, k: (l, k, 0, 0)),
        ],
        out_specs=pl.BlockSpec((None, None, SUBLANES, D_MODEL), lambda l, k: (l, k, 0, 0)),
        out_shape=jax.ShapeDtypeStruct((DEPTH, N_MOD, SUBLANES, D_MODEL), F32),
        compiler_params=_params(("arbitrary", "arbitrary")),
        name="adaln",
    )(cond8, w_mod, b_mod.reshape(DEPTH, N_MOD, 1, D_MODEL))
    return out.reshape(DEPTH, N_MOD, SUBLANES, 1, D_MODEL)


def _mod_spec(layer, k, tile_rows=TM):
    per = TM // tile_rows
    return pl.BlockSpec((None, None, None, 1, D_MODEL),
                        lambda i, *_: (layer, k, _mod_row(i // per), 0, 0))


def _ng_spec(layer, which):
    return pl.BlockSpec((None, None, 1, D_MODEL), lambda i, *_: (layer, which, 0, 0))


def _modulate_seg_major(x_ref, ng_ref, sh_ref, sc_ref, perm_scr, h_scr):
    for r in range(N_SEG):
        rows = slice(r * SEG, (r + 1) * SEG)
        h = _modulate(x_ref[rows, :], ng_ref[...], sh_ref[...], sc_ref[...])
        for cb in range(D_MODEL // LANES):
            perm_scr[cb, pl.ds(r, SEG, stride=N_SEG), :] = h[:, cb * LANES:(cb + 1) * LANES]
    for cb in range(D_MODEL // LANES):
        h_scr[:, cb * LANES:(cb + 1) * LANES] = perm_scr[cb].astype(BF16)


def _residual_from_seg_major(x_ref, gt_ref, perm_scr, o_ref):
    for r in range(N_SEG):
        rows = slice(r * SEG, (r + 1) * SEG)
        for cb in range(D_MODEL // LANES):
            cols = slice(cb * LANES, (cb + 1) * LANES)
            y = perm_scr[cb, pl.ds(r, SEG, stride=N_SEG), :]
            o_ref[rows, cols] = x_ref[rows, cols] + gt_ref[:, cols] * y


def _seg_masks(i, width):
    per_seq = jnp.where(i < N_PROMPT_TILES, SEQ // SEG, DEC_SEQ // SEG)
    r = lax.broadcasted_iota(jnp.int32, (N_SEG, width), 0) & (per_seq - 1)
    return r == 0, r == per_seq - 1


def _shift_prev(z, is_start):
    wrap = jnp.where(is_start, 0.0, pltpu.roll(z[TM - N_SEG:, :], 1, 0))
    return jnp.concatenate([wrap, z[:TM - N_SEG, :]], axis=0)


def _shift_next(z, is_end):
    wrap = jnp.where(is_end, 0.0, pltpu.roll(z[:N_SEG, :], N_SEG - 1, 0))
    return jnp.concatenate([z[N_SEG:, :], wrap], axis=0)


def _ffn_kernel(x_ref, sh_ref, sc_ref, gt_ref, ng_ref, wg_ref, wu_ref, cwg_ref, cwu_ref,
                cbg_ref, cbu_ref, wd_ref, o_ref, h_scr, acc_scr, w_scr, z_scr, act_scr):
    i = pl.program_id(0)
    j = pl.program_id(1)

    @pl.when(j == 0)
    def _():
        _modulate_seg_major(x_ref, ng_ref, sh_ref, sc_ref, acc_scr, h_scr)
        acc_scr[...] = jnp.zeros_like(acc_scr)

    is_start, is_end = _seg_masks(i, 2 * FF_TN)
    w_scr[:, :FF_TN] = wg_ref[...].astype(BF16)
    w_scr[:, FF_TN:] = wu_ref[...].astype(BF16)
    cw = jnp.concatenate([cwg_ref[...], cwu_ref[...]], axis=1)
    cb = jnp.concatenate([cbg_ref[...], cbu_ref[...]], axis=1)

    def up(m):
        r0 = m * FF_BLK
        z_scr[N_SEG + r0:N_SEG + r0 + FF_BLK, :] = _dot(h_scr[r0:r0 + FF_BLK, :], w_scr[...])

    def post(m):
        r0 = N_SEG + m * FF_BLK
        c = (cb + z_scr[r0 - N_SEG:r0 - N_SEG + FF_BLK, :] * cw[0:1, :] + z_scr[r0:r0 + FF_BLK, :] * cw[1:2, :]
             + z_scr[r0 + N_SEG:r0 + N_SEG + FF_BLK, :] * cw[2:3, :])
        hg = 0.5 * c[:, :FF_TN]
        act = (hg * (jnp.tanh(hg) + 1.0)) * c[:, FF_TN:]
        act_scr[m * FF_BLK:(m + 1) * FF_BLK, :] = act.astype(BF16)

    n_blk = TM // FF_BLK
    up(n_blk - 1)
    z_scr[0:N_SEG, :] = jnp.where(is_start, 0.0, pltpu.roll(z_scr[TM:TM + N_SEG, :], 1, 0))
    up(0)
    z_scr[TM + N_SEG:, :] = jnp.where(is_end, 0.0, pltpu.roll(z_scr[N_SEG:2 * N_SEG, :], N_SEG - 1, 0))
    up(1)
    for m in range(n_blk):
        if m + 2 < n_blk - 1:
            up(m + 2)
        post(m)
    y = _dot(act_scr[...], wd_ref[...].astype(BF16))
    for cb_i in range(D_MODEL // LANES):
        acc_scr[cb_i] += y[:, cb_i * LANES:(cb_i + 1) * LANES]

    @pl.when(j == N_FF_CHUNKS - 1)
    def _():
        _residual_from_seg_major(x_ref, gt_ref, acc_scr, o_ref)


N_FFN_INPUTS = 12


def _ffn_kernel_split_out(*refs):
    ins = refs[:N_FFN_INPUTS]
    op_ref, os_ref = refs[N_FFN_INPUTS:N_FFN_INPUTS + 2]
    *scratch, o_scr = refs[N_FFN_INPUTS + 2:]
    _ffn_kernel(*ins, o_scr, *scratch)
    last = pl.program_id(1) == N_FF_CHUNKS - 1
    is_prompt = pl.program_id(0) < N_PROMPT_TILES

    @pl.when(last & is_prompt)
    def _():
        op_ref[...] = o_scr[...]

    @pl.when(last & jnp.logical_not(is_prompt))
    def _():
        os_ref[...] = o_scr[...]


def _ffn(x, mod, layer, norm_g4, w_up, conv_w, conv_b, w_down, split_out=False):
    conv_b3 = conv_b.reshape(DEPTH, 1, 2 * D_FF)
    scratch = [pltpu.VMEM((TM, D_MODEL), BF16), pltpu.VMEM((D_MODEL // LANES, TM, LANES), F32),
               pltpu.VMEM((D_MODEL, 2 * FF_TN), BF16), pltpu.VMEM((TM + 2 * N_SEG, 2 * FF_TN), F32),
               pltpu.VMEM((TM, FF_TN), BF16)]
    if split_out:
        body = _ffn_kernel_split_out
        out_specs = [pl.BlockSpec((TM, D_MODEL), lambda i, j: (jnp.minimum(i, N_PROMPT_TILES - 1), 0)),
                     pl.BlockSpec((TM, D_MODEL), lambda i, j: (jnp.maximum(i - N_PROMPT_TILES, 0), 0))]
        out_shape = [jax.ShapeDtypeStruct((N_PROMPT, D_MODEL), F32), jax.ShapeDtypeStruct((N_SAMPLE, D_MODEL), F32)]
        scratch = scratch + [pltpu.VMEM((TM, D_MODEL), F32)]
    else:
        body = _ffn_kernel
        out_specs = pl.BlockSpec((TM, D_MODEL), lambda i, j: (i, 0))
        out_shape = jax.ShapeDtypeStruct((N_ROWS, D_MODEL), F32)
    return pl.pallas_call(
        body,
        grid=(N_TILES, N_FF_CHUNKS),
        in_specs=[
            pl.BlockSpec((TM, D_MODEL), lambda i, j: (i, 0)),
            _mod_spec(layer, 3), _mod_spec(layer, 4), _mod_spec(layer, 5),
            _ng_spec(layer, 1),
            pl.BlockSpec((None, D_MODEL, FF_TN), lambda i, j: (layer, 0, j)),
            pl.BlockSpec((None, D_MODEL, FF_TN), lambda i, j: (layer, 0, N_FF_CHUNKS + j)),
            pl.BlockSpec((None, FFN_CONV_W, FF_TN), lambda i, j: (layer, 0, j)),
            pl.BlockSpec((None, FFN_CONV_W, FF_TN), lambda i, j: (layer, 0, N_FF_CHUNKS + j)),
            pl.BlockSpec((None, 1, FF_TN), lambda i, j: (layer, 0, j)),
            pl.BlockSpec((None, 1, FF_TN), lambda i, j: (layer, 0, N_FF_CHUNKS + j)),
            pl.BlockSpec((None, FF_TN, D_MODEL), lambda i, j: (layer, j, 0)),
        ],
        out_specs=out_specs,
        out_shape=out_shape,
        scratch_shapes=scratch,
        compiler_params=_params(("arbitrary", "arbitrary")),
        name=f"ffn{layer}",
    )(x, mod, mod, mod, norm_g4, w_up, w_up, conv_w, conv_w, conv_b3, conv_b3, w_down)


def _lru_kernel(x_ref, sh_ref, sc_ref, gt_ref, ng_ref, wig_ref, wir_ref, cw_ref, cb_ref, wgt_ref, bgt_ref,
                lam_ref, h0_ref, wout_ref, o_ref, st_ref,
                h_scr, yg_scr, perm_scr, gg_scr, af_scr, bf_scr, ab_scr, bb_scr):
    i = pl.program_id(0)
    n = pl.program_id(1)

    @pl.when(n == 0)
    def _():
        _modulate_seg_major(x_ref, ng_ref, sh_ref, sc_ref, perm_scr, h_scr)

    h = h_scr[...]
    is_start, is_end = _seg_masks(i, LRU_TN)
    gg_scr[...] = _gelu(_dot(h, wig_ref[...].astype(BF16)))
    rec = _dot(h, wir_ref[...].astype(BF16))
    prev1 = _shift_prev(rec, is_start)
    prev2 = _shift_prev(prev1, is_start)
    xc = (cb_ref[...] + prev2 * cw_ref[0:1, :] + prev1 * cw_ref[1:2, :] + rec * cw_ref[2:3, :]
          + _shift_next(rec, is_end) * cw_ref[3:4, :])

    lam = lam_ref[...]
    softplus = jnp.maximum(-lam, 0.0) + jnp.log1p(jnp.exp(-jnp.abs(lam)))
    for e in range(LRU_TN // LRU_BLOCK):
        cols = slice(e * LRU_BLOCK, (e + 1) * LRU_BLOCK)
        xce = xc[:, cols]
        w4 = jnp.concatenate([wgt_ref[d, g, e] for d in range(2) for g in range(2)], axis=1)
        b4 = jnp.concatenate([bgt_ref[d, g, e] for d in range(2) for g in range(2)], axis=1)
        gates = _sigmoid(_dot(xce.astype(BF16), w4.astype(BF16)) + b4)
        for d, (a_scr, b_scr) in enumerate(((af_scr, bf_scr), (ab_scr, bb_scr))):
            r = gates[:, (2 * d) * LRU_BLOCK:(2 * d + 1) * LRU_BLOCK]
            ig = gates[:, (2 * d + 1) * LRU_BLOCK:(2 * d + 2) * LRU_BLOCK]
            log_a = (-LRU_C * r) * softplus[d:d + 1, cols]
            a = jnp.exp(log_a)
            a_scr[:, cols] = a
            b_scr[:, cols] = jnp.sqrt(jnp.maximum(1.0 - a * a, 0.0)) * (ig * xce)

    def put(k, row, v):
        perm_scr[2 * k, pl.ds(row, N_SEG), :] = v[:, :LANES]
        perm_scr[2 * k + 1, pl.ds(row, N_SEG), :] = v[:, LANES:]

    def get(k):
        return jnp.concatenate([perm_scr[2 * k], perm_scr[2 * k + 1]], axis=1)

    def local_scan(t, carry):
        hf, cf, hb, cb = carry
        rf = pl.multiple_of(t * N_SEG, N_SEG)
        rb = pl.multiple_of((SEG - 1 - t) * N_SEG, N_SEG)
        a = af_scr[pl.ds(rf, N_SEG), :]
        hf = a * hf + bf_scr[pl.ds(rf, N_SEG), :]
        cf = a * cf
        put(0, rf, hf)
        put(1, rf, cf)
        a = ab_scr[pl.ds(rb, N_SEG), :]
        hb = a * hb + bb_scr[pl.ds(rb, N_SEG), :]
        cb = a * cb
        put(2, rb, hb)
        put(3, rb, cb)
        return hf, cf, hb, cb

    zeros = jnp.zeros((N_SEG, LRU_TN), F32)
    ones = jnp.ones((N_SEG, LRU_TN), F32)
    end_f, dec_f, end_b, dec_b = lax.fori_loop(0, SEG, local_scan, (zeros, ones, zeros, ones), unroll=8)

    per_seq = jnp.where(i < N_PROMPT_TILES, SEQ // SEG, DEC_SEQ // SEG)
    sub = lax.broadcasted_iota(jnp.int32, (N_SEG, LRU_TN), 0)
    h0f = h0_ref[0:1, :]
    h0b = h0_ref[1:2, :]
    in_f, in_b = zeros, zeros
    out_f, out_b = [None] * N_SEG, [None] * N_SEG
    for r in range(N_SEG):
        hin = h0f if r == 0 else jnp.where((r & (per_seq - 1)) == 0, h0f, out_f[r - 1])
        in_f = jnp.where(sub == r, hin, in_f)
        out_f[r] = end_f[r:r + 1, :] + dec_f[r:r + 1, :] * hin
    for r in reversed(range(N_SEG)):
        hin = h0b if r == N_SEG - 1 else jnp.where((r & (per_seq - 1)) == per_seq - 1, h0b, out_b[r + 1])
        in_b = jnp.where(sub == r, hin, in_b)
        out_b[r] = end_b[r:r + 1, :] + dec_b[r:r + 1, :] * hin
    for q in range(SEQ_PER_TILE):
        st_ref[q, 0:1, :] = out_f[2 * q + 1]
        st_ref[q, 1:2, :] = out_b[2 * q]

    def every_step(v):
        return jnp.broadcast_to(v[None], (SEG, N_SEG, LRU_TN)).reshape(TM, LRU_TN)

    hsum = ((get(0) + get(1) * every_step(in_f))
            + (get(2) + get(3) * every_step(in_b)))
    col = pl.multiple_of(n * LRU_TN, LRU_TN)
    yg_scr[:, pl.ds(col, LRU_TN)] = (gg_scr[...] * hsum).astype(BF16)

    @pl.when(n == D_RNN // LRU_TN - 1)
    def _():
        y = _dot(yg_scr[...], wout_ref[...].astype(BF16))
        for cb in range(D_MODEL // LANES):
            perm_scr[cb] = y[:, cb * LANES:(cb + 1) * LANES]
        _residual_from_seg_major(x_ref, gt_ref, perm_scr, o_ref)


def _lru_kernel_split_in(xp_ref, xs_ref, *refs):
    *rest, x_scr = refs
    first = pl.program_id(1) == 0
    is_prompt = pl.program_id(0) < N_PROMPT_TILES

    @pl.when(first & is_prompt)
    def _():
        x_scr[...] = xp_ref[...]

    @pl.when(first & jnp.logical_not(is_prompt))
    def _():
        x_scr[...] = xs_ref[...]

    _lru_kernel(x_scr, *rest)


def _lru_mixer(x, mod, layer, j, norm_g4, state_lru, w_in, conv_w, conv_b, w_gate, b_gate, lam, w_out):
    n_steps = D_RNN // LRU_TN
    per_step = LRU_TN // LRU_BLOCK
    n_lru = w_in.shape[0]
    h0 = jnp.concatenate([jnp.zeros((N_PROMPT_TILES, 2, D_RNN), F32), state_lru[:, j]], axis=0)
    scratch = ([pltpu.VMEM((TM, D_MODEL), BF16), pltpu.VMEM((TM, D_RNN), BF16),
                pltpu.VMEM((D_MODEL // LANES, TM, LANES), F32)] + [pltpu.VMEM((TM, LRU_TN), F32)] * 5)
    if isinstance(x, tuple):
        body = _lru_kernel_split_in
        x_args = x
        x_specs = [pl.BlockSpec((TM, D_MODEL), lambda i, n: (jnp.minimum(i, N_PROMPT_TILES - 1), 0)),
                   pl.BlockSpec((TM, D_MODEL), lambda i, n: (jnp.maximum(i - N_PROMPT_TILES, 0), 0))]
        scratch = scratch + [pltpu.VMEM((TM, D_MODEL), F32)]
    else:
        body = _lru_kernel
        x_args = (x,)
        x_specs = [pl.BlockSpec((TM, D_MODEL), lambda i, n: (i, 0))]
    out, st = pl.pallas_call(
        body,
        grid=(N_TILES, n_steps),
        in_specs=x_specs + [
            _mod_spec(layer, 0), _mod_spec(layer, 1), _mod_spec(layer, 2),
            _ng_spec(layer, 0),
            pl.BlockSpec((None, D_MODEL, LRU_TN), lambda i, n: (j, 0, n)),
            pl.BlockSpec((None, D_MODEL, LRU_TN), lambda i, n: (j, 0, n_steps + n)),
            pl.BlockSpec((None, LRU_CONV_W, LRU_TN), lambda i, n: (j, 0, n)),
            pl.BlockSpec((None, 1, LRU_TN), lambda i, n: (j, 0, n)),
            pl.BlockSpec((None, 2, 2, per_step, LRU_BLOCK, LRU_BLOCK), lambda i, n: (j, 0, 0, n, 0, 0)),
            pl.BlockSpec((None, 2, 2, per_step, 1, LRU_BLOCK), lambda i, n: (j, 0, 0, n, 0, 0)),
            pl.BlockSpec((None, 2, LRU_TN), lambda i, n: (j, 0, n)),
            pl.BlockSpec((None, 2, LRU_TN), lambda i, n: (i, 0, n)),
            pl.BlockSpec((None, D_RNN, D_MODEL), lambda i, n: (j, 0, 0), pipeline_mode=pl.Buffered(1)),
        ],
        out_specs=[
            pl.BlockSpec((TM, D_MODEL), lambda i, n: (i, 0)),
            pl.BlockSpec((SEQ_PER_TILE, 2, LRU_TN), lambda i, n: (i, 0, n)),
        ],
        out_shape=[
            jax.ShapeDtypeStruct((N_ROWS, D_MODEL), F32),
            jax.ShapeDtypeStruct((N_TILES * SEQ_PER_TILE, 2, D_RNN), F32),
        ],
        scratch_shapes=scratch,
        compiler_params=_params(("arbitrary", "arbitrary")),
        name=f"lru{layer}",
    )(*x_args, mod, mod, mod, norm_g4, w_in, w_in, conv_w, conv_b.reshape(n_lru, 1, D_RNN), w_gate,
      b_gate.reshape(n_lru, 2, 2, LRU_BLOCKS, 1, LRU_BLOCK), lam, h0, w_out)
    return out, st[:BATCH]


def _cmlp_kernel(x_ref, sh_ref, sc_ref, gt_ref, ng_ref, win_ref, bin_ref, vng_ref, ws_ref, bs_ref,
                 wout_ref, o_ref, u_scr, v_scr, uv_scr):
    gw = D_B // G_B
    h = _modulate(x_ref[...], ng_ref[...], sh_ref[...], sc_ref[...]).astype(BF16)
    ssq = jnp.zeros((TM_B, 1), F32)
    for g in range(G_B):
        cu = slice(g * gw, (g + 1) * gw)
        cv = slice(D_B + g * gw, D_B + (g + 1) * gw)
        u_scr[:, cu] = _gelu(_dot(h, win_ref[:, cu]) + bin_ref[:, cu])
        v = _gelu(_dot(h, win_ref[:, cv]) + bin_ref[:, cv])
        v_scr[:, cu] = v
        ssq = ssq + jnp.sum(v * v, axis=-1, keepdims=True)
    rinv = lax.rsqrt(ssq * (1.0 / D_B) + EPS)
    for g in range(G_B):
        cu = slice(g * gw, (g + 1) * gw)
        vn = ((v_scr[:, cu] * rinv) * vng_ref[:, cu]).astype(BF16)
        w_s = ws_ref[g]
        for c in range(TM_B // CHUNK):
            rows = slice(c * CHUNK, (c + 1) * CHUNK)
            sv = _dot(w_s, vn[rows, :]) + bs_ref[:, cu]
            uv_scr[rows, cu] = (u_scr[rows, cu] * sv).astype(BF16)
    o_ref[...] = x_ref[...] + gt_ref[...] * _dot(uv_scr[...], wout_ref[...])


def _cmlp_mixer(x, mod, layer, ng, w_in, b_in, vnorm_g, w_s, b_s, w_out):
    gw = D_B // G_B
    bs_full = jnp.repeat(b_s.T, gw, axis=1)
    const = lambda *shape: pl.BlockSpec(shape, lambda i: (0,) * len(shape), pipeline_mode=pl.Buffered(1))
    return pl.pallas_call(
        _cmlp_kernel,
        grid=(N_ROWS // TM_B,),
        in_specs=[
            pl.BlockSpec((TM_B, D_MODEL), lambda i: (i, 0)),
            _mod_spec(layer, 0, TM_B), _mod_spec(layer, 1, TM_B), _mod_spec(layer, 2, TM_B),
            const(1, D_MODEL),
            const(D_MODEL, 2 * D_B),
            const(1, 2 * D_B),
            const(1, D_B),
            const(G_B, CHUNK, CHUNK),
            const(CHUNK, D_B),
            const(D_B, D_MODEL),
        ],
        out_specs=pl.BlockSpec((TM_B, D_MODEL), lambda i: (i, 0)),
        out_shape=jax.ShapeDtypeStruct((N_ROWS, D_MODEL), F32),
        scratch_shapes=[pltpu.VMEM((TM_B, D_B), F32), pltpu.VMEM((TM_B, D_B), F32),
                        pltpu.VMEM((TM_B, D_B), BF16)],
        compiler_params=_params(("arbitrary",)),
        name=f"cmlp{layer}",
    )(x, mod, mod, mod, ng.reshape(1, D_MODEL), w_in.astype(BF16), b_in.reshape(1, 2 * D_B),
      vnorm_g.reshape(1, D_B), w_s.astype(BF16), bs_full, w_out.astype(BF16))


def _qkv_kernel(x_ref, sh_ref, sc_ref, ng_ref, w_ref, qkg_ref, gsum_ref, cos_ref, sin_ref,
                q_ref, kp_ref, ks_ref, vp_ref, vs_ref):
    i = pl.program_id(0)
    h = _modulate(x_ref[...], ng_ref[...], sh_ref[...], sc_ref[...]).astype(BF16)
    lane = lax.broadcasted_iota(jnp.int32, (TM_Q, LANES), 1)
    first_half = (lane & (HEAD_DIM_C - 1)) < HEAD_DIM_C // 2

    def norm_store(part, out_ref, rope, out_scale):
        z = _dot(h, w_ref[:, part * D_MODEL:(part + 1) * D_MODEL])
        for hb in range(N_HEADS_C):
            zb = z[:, hb * LANES:(hb + 1) * LANES]
            sq = zb * zb
            hi = sq.astype(BF16)
            lo = (sq - hi.astype(F32)).astype(BF16)
            ssum = _dot(hi, gsum_ref[...]) + _dot(lo, gsum_ref[...])
            zb = zb * lax.rsqrt(ssum * (1.0 / HEAD_DIM_C) + EPS) * qkg_ref[part:part + 1, :]
            if rope:
                partner = jnp.where(first_half, pltpu.roll(zb, LANES - HEAD_DIM_C // 2, 1),
                                    pltpu.roll(zb, HEAD_DIM_C // 2, 1))
                zb = zb * cos_ref[...] + partner * sin_ref[...]
            if out_scale != 1.0:
                zb = zb * out_scale
            out_ref[:, hb * LANES:(hb + 1) * LANES] = zb.astype(out_ref.dtype)

    def qkv(rope, k_ref, v_ref):
        norm_store(0, q_ref, rope, HEAD_DIM_C ** -0.5)
        norm_store(1, k_ref, rope, 1.0)
        v_ref[...] = _dot(h, w_ref[:, 2 * D_MODEL:])

    @pl.when(i < N_PROMPT // TM_Q)
    def _():
        qkv(False, kp_ref, vp_ref)

    @pl.when(i >= N_PROMPT // TM_Q)
    def _():
        qkv(True, ks_ref, vs_ref)


def _attn_heads(q_ref, k_ref, v_ref, lam_ref, sg_ref, o_ref, lam_init):
    lv = lam_ref[...]
    lam = (jnp.exp(jnp.sum(lv[0:1, :] * lv[1:2, :], axis=-1, keepdims=True))
           - jnp.exp(jnp.sum(lv[2:3, :] * lv[3:4, :], axis=-1, keepdims=True)) + lam_init)
    lane = lax.broadcasted_iota(jnp.int32, (TQ, LANES), 1)
    comp0 = lane < HEAD_DIM_C
    for hd in range(N_HEADS_C):
        cols = slice(hd * LANES, (hd + 1) * LANES)
        qh = q_ref[:, cols]
        kh = k_ref[:, cols]
        vh = v_ref[:, cols]
        zero = jnp.zeros_like(qh)
        s0 = _dot_nt(jnp.where(comp0, qh, zero), kh)
        s1 = _dot_nt(jnp.where(comp0, zero, qh), kh)
        e0 = jnp.exp(s0 - jnp.max(s0, axis=-1, keepdims=True))
        e1 = jnp.exp(s1 - jnp.max(s1, axis=-1, keepdims=True))
        p0 = e0 / jnp.sum(e0, axis=-1, keepdims=True)
        p1 = e1 / jnp.sum(e1, axis=-1, keepdims=True)
        w = p0 - lam * p1
        o = _dot(w.astype(BF16), vh)
        o = o * lax.rsqrt(jnp.mean(o * o, axis=-1, keepdims=True) + EPS) * sg_ref[...]
        o_ref[:, cols] = (o * (1.0 - lam_init)).astype(BF16)


def _attn_prompt_kernel(q_ref, k_ref, v_ref, lam_ref, sg_ref, x_ref, gt_ref, w_ref, y_ref, k_scr, v_scr, o_scr,
                        *, lam_init):
    k_scr[...] = k_ref[...].astype(BF16)
    v_scr[...] = v_ref[...].astype(BF16)
    _attn_heads(q_ref, k_scr, v_scr, lam_ref, sg_ref, o_scr, lam_init)
    y_ref[...] = x_ref[...] + gt_ref[...] * _dot(o_scr[...], w_ref[...])


def _attn_sample_kernel(q_ref, kl_ref, vl_ref, kc_ref, vc_ref, lam_ref, sg_ref, x_ref, gt_ref, w_ref, y_ref,
                        k_scr, v_scr, o_scr, *, lam_init):
    @pl.when(pl.program_id(1) == 0)
    def _():
        k_scr[0:PAST_LEN, :] = kc_ref[...].astype(BF16)
        k_scr[PAST_LEN:, :] = kl_ref[...].astype(BF16)
        v_scr[0:PAST_LEN, :] = vc_ref[...].astype(BF16)
        v_scr[PAST_LEN:, :] = vl_ref[...].astype(BF16)

    _attn_heads(q_ref, k_scr, v_scr, lam_ref, sg_ref, o_scr, lam_init)
    y_ref[...] = x_ref[...] + gt_ref[...] * _dot(o_scr[...], w_ref[...])


def _rope_tables():
    t = jnp.arange(DEC_SEQ)
    inv = ROPE_BASE ** (-jnp.arange(N_FREQ_AXIS, dtype=F32) / N_FREQ_AXIS)
    ang = jnp.concatenate([(t // GRID_W)[:, None] * inv, (t % GRID_W)[:, None] * inv], axis=-1)
    cos, sin = jnp.cos(ang), jnp.sin(ang)
    cos64 = jnp.concatenate([cos, cos], axis=-1)
    sin64 = jnp.concatenate([-sin, sin], axis=-1)
    return jnp.tile(cos64, (1, 2)), jnp.tile(sin64, (1, 2))


def _attn_mixer(x, mod, layer, ng, cache_k, cache_v, w_qkv, qk_g, lam_vec, subln_g, w_out):
    lam_init = 0.8 - 0.6 * math.exp(-0.3 * layer)
    cos_t, sin_t = _rope_tables()
    qkg = jnp.tile(qk_g, (1, 2))
    lane = jnp.arange(LANES)
    gsum = (lane[:, None] // HEAD_DIM_C == lane[None, :] // HEAD_DIM_C).astype(BF16)
    n_prompt_q = N_PROMPT // TM_Q
    per_seq = DEC_SEQ // TM_Q
    prm = lambda i: (jnp.minimum(i, n_prompt_q - 1), 0)
    smp = lambda i: (jnp.maximum(i - n_prompt_q, 0), 0)
    rope_blk = lambda i: (i % per_seq, 0)
    q, kp, ks, vp, vs = pl.pallas_call(
        _qkv_kernel,
        grid=(N_ROWS // TM_Q,),
        in_specs=[
            pl.BlockSpec((TM_Q, D_MODEL), lambda i: (i, 0)),
            _mod_spec(layer, 0, TM_Q), _mod_spec(layer, 1, TM_Q),
            pl.BlockSpec((1, D_MODEL), lambda i: (0, 0)),
            pl.BlockSpec((D_MODEL, 3 * D_MODEL), lambda i: (0, 0), pipeline_mode=pl.Buffered(1)),
            pl.BlockSpec((2, LANES), lambda i: (0, 0)),
            pl.BlockSpec((LANES, LANES), lambda i: (0, 0)),
            pl.BlockSpec((TM_Q, LANES), rope_blk),
            pl.BlockSpec((TM_Q, LANES), rope_blk),
        ],
        out_specs=[
            pl.BlockSpec((TM_Q, D_MODEL), lambda i: (i, 0)),
            pl.BlockSpec((TM_Q, D_MODEL), prm), pl.BlockSpec((TM_Q, D_MODEL), smp),
            pl.BlockSpec((TM_Q, D_MODEL), prm), pl.BlockSpec((TM_Q, D_MODEL), smp),
        ],
        out_shape=[
            jax.ShapeDtypeStruct((N_ROWS, D_MODEL), BF16),
            jax.ShapeDtypeStruct((N_PROMPT, D_MODEL), F32), jax.ShapeDtypeStruct((N_SAMPLE, D_MODEL), F32),
            jax.ShapeDtypeStruct((N_PROMPT, D_MODEL), F32), jax.ShapeDtypeStruct((N_SAMPLE, D_MODEL), F32),
        ],
        compiler_params=_params(("arbitrary",)),
        name=f"qkv{layer}",
    )(x, mod, mod, ng.reshape(1, D_MODEL), w_qkv.astype(BF16), qkg, gsum, cos_t, sin_t)

    sg = subln_g.reshape(1, 2 * HEAD_DIM_C)
    w_out_bf = w_out.astype(BF16)
    gate = lambda row_fn: pl.BlockSpec((None, None, None, 1, D_MODEL), row_fn)
    x_mid = pl.pallas_call(
        functools.partial(_attn_prompt_kernel, lam_init=lam_init),
        grid=(BATCH,),
        in_specs=[
            pl.BlockSpec((SEQ, D_MODEL), lambda b: (b, 0)),
            pl.BlockSpec((SEQ, D_MODEL), lambda b: (b, 0)),
            pl.BlockSpec((SEQ, D_MODEL), lambda b: (b, 0)),
            pl.BlockSpec((4, HEAD_DIM_C), lambda b: (0, 0)),
            pl.BlockSpec((1, 2 * HEAD_DIM_C), lambda b: (0, 0)),
            pl.BlockSpec((SEQ, D_MODEL), lambda b: (b, 0)),
            gate(lambda b: (layer, 2, 0, 0, 0)),
            pl.BlockSpec((D_MODEL, D_MODEL), lambda b: (0, 0), pipeline_mode=pl.Buffered(1)),
        ],
        out_specs=pl.BlockSpec((SEQ, D_MODEL), lambda b: (b, 0)),
        out_shape=jax.ShapeDtypeStruct((N_ROWS, D_MODEL), F32),
        input_output_aliases={5: 0},
        scratch_shapes=[pltpu.VMEM((SEQ, D_MODEL), BF16), pltpu.VMEM((SEQ, D_MODEL), BF16),
                        pltpu.VMEM((SEQ, D_MODEL), BF16)],
        compiler_params=_params(("arbitrary",)),
        name=f"attn_prompt{layer}",
    )(q, kp, vp, lam_vec, sg, x, mod, w_out_bf)

    n_qb = DEC_SEQ // TQ
    t_all = PAST_LEN + DEC_SEQ
    lat_blk = lambda b, t: (N_PROMPT // TQ + b * n_qb + t, 0)
    x_new = pl.pallas_call(
        functools.partial(_attn_sample_kernel, lam_init=lam_init),
        grid=(DEC_BATCH, n_qb),
        in_specs=[
            pl.BlockSpec((TQ, D_MODEL), lat_blk),
            pl.BlockSpec((DEC_SEQ, D_MODEL), lambda b, t: (b, 0)),
            pl.BlockSpec((DEC_SEQ, D_MODEL), lambda b, t: (b, 0)),
            pl.BlockSpec((None, PAST_LEN, D_MODEL), lambda b, t: (b, 0, 0)),
            pl.BlockSpec((None, PAST_LEN, D_MODEL), lambda b, t: (b, 0, 0)),
            pl.BlockSpec((4, HEAD_DIM_C), lambda b, t: (0, 0)),
            pl.BlockSpec((1, 2 * HEAD_DIM_C), lambda b, t: (0, 0)),
            pl.BlockSpec((TQ, D_MODEL), lat_blk),
            gate(lambda b, t: (layer, 2, 1 + b, 0, 0)),
            pl.BlockSpec((D_MODEL, D_MODEL), lambda b, t: (0, 0), pipeline_mode=pl.Buffered(1)),
        ],
        out_specs=pl.BlockSpec((TQ, D_MODEL), lat_blk),
        out_shape=jax.ShapeDtypeStruct((N_ROWS, D_MODEL), F32),
        input_output_aliases={7: 0},
        scratch_shapes=[pltpu.VMEM((t_all, D_MODEL), BF16), pltpu.VMEM((t_all, D_MODEL), BF16),
                        pltpu.VMEM((TQ, D_MODEL), BF16)],
        compiler_params=_params(("arbitrary", "arbitrary")),
        name=f"attn_sample{layer}",
    )(q, ks, vs, cache_k.reshape(DEC_BATCH, PAST_LEN, D_MODEL), cache_v.reshape(DEC_BATCH, PAST_LEN, D_MODEL),
      lam_vec, sg, x_mid, mod, w_out_bf)
    return x_new, kp, vp


def kernel(x_prompt, x_sample, state_lru, cache_k, cache_v, c, c_ctx, w_mod, b_mod, norm_g, lru_w_in, lru_conv_w, lru_conv_b, lru_w_gate, lru_b_gate, lru_lambda, lru_w_out, cmlp_w_in, cmlp_b_in, cmlp_norm_g, cmlp_w_s, cmlp_b_s, cmlp_w_out, attn_w_qkv, attn_qk_g, attn_lambda, attn_subln_g, attn_w_out, ffn_w_up, ffn_conv_w, ffn_conv_b, ffn_w_down):
    x = (x_prompt.reshape(N_PROMPT, D_MODEL), x_sample.reshape(N_SAMPLE, D_MODEL))
    cond8 = jnp.concatenate([c_ctx[None, :], c, jnp.zeros((SUBLANES - 1 - DEC_BATCH, D_MODEL), F32)], axis=0)
    mod = _adaln(cond8, w_mod, b_mod)
    norm_g4 = norm_g.reshape(DEPTH, 2, 1, D_MODEL)
    new_lru, new_k, new_v = [], [], []
    for l in range(DEPTH):
        kind, j = l % 3, l // 3
        if kind == 0:
            x, st = _lru_mixer(x, mod, l, j, norm_g4, state_lru, lru_w_in, lru_conv_w, lru_conv_b,
                               lru_w_gate, lru_b_gate, lru_lambda, lru_w_out)
            new_lru.append(st)
        elif kind == 1:
            x = _cmlp_mixer(x, mod, l, norm_g[l, 0], cmlp_w_in[j], cmlp_b_in[j], cmlp_norm_g[j],
                            cmlp_w_s[j], cmlp_b_s[j], cmlp_w_out[j])
        else:
            x, kp, vp = _attn_mixer(x, mod, l, norm_g[l, 0], cache_k[:, j], cache_v[:, j], attn_w_qkv[j],
                                    attn_qk_g[j], attn_lambda[j], attn_subln_g[j], attn_w_out[j])
            new_k.append(kp.reshape(BATCH, SEQ, N_HEADS_C, 2, HEAD_DIM_C))
            new_v.append(vp.reshape(BATCH, SEQ, N_HEADS_C, 2 * HEAD_DIM_C))
        x = _ffn(x, mod, l, norm_g4, ffn_w_up, ffn_conv_w, ffn_conv_b, ffn_w_down, split_out=(l == DEPTH - 1))
    y_prompt = x[0].reshape(BATCH, SEQ, D_MODEL)
    y_sample = x[1].reshape(DEC_BATCH, DEC_SEQ, D_MODEL)
    return (y_prompt, y_sample, jnp.stack(new_lru, axis=1), jnp.stack(new_k, axis=1), jnp.stack(new_v, axis=1))
```
